```python
import jax, jax.numpy as jnp
from jax import lax
import numpy as np

D_MODEL = 1024
BATCH = 2
SEQ = 8192
DEPTH = 2

CHUNK = 64
M_HEADS = 4
M_WIDTH = D_MODEL // 2
M_HEAD_DIM = M_WIDTH // M_HEADS
CONV_K = 4
P_GROUPS = 4
P_WIDTH = D_MODEL // 4
P_GROUP_DIM = P_WIDTH // P_GROUPS
POOL_WINDOWS = (2, 4, 8, 16)
S_HEADS = 4
S_WIDTH = D_MODEL // 4
S_HEAD_DIM = S_WIDTH // S_HEADS
SB_BLOCK = 128
N_BRANCH = 3
N_IN = 5 * M_WIDTH + 2 * M_HEADS + 2 * P_WIDTH + 4 * S_WIDTH + N_BRANCH * D_MODEL
EPS = 1e-6

kernel_name = "hybrid_mlstm_pool_stickbreak_block"


def _split_points():
    sizes = (M_WIDTH, M_WIDTH, M_WIDTH, M_HEADS, M_HEADS, M_WIDTH, M_WIDTH,
             P_WIDTH, P_WIDTH, S_WIDTH, S_WIDTH, S_WIDTH, S_WIDTH)
    pts, acc = [], 0
    for s in sizes:
        acc += s
        pts.append(acc)
    return pts


def _rmsnorm(x, g):
    xf = x.astype(jnp.float32)
    y = xf * lax.rsqrt(jnp.mean(xf * xf, axis=-1, keepdims=True) + EPS)
    return (y * g.astype(jnp.float32)).astype(x.dtype)


def _causal_conv(x, w, b):
    k_w = w.shape[0]
    s = x.shape[1]
    xp = jnp.pad(x, ((0, 0), (k_w - 1, 0), (0, 0)))
    return sum(xp[:, j:j + s] * w[j] for j in range(k_w)) + b


def _mlstm_chunkwise(q, k, v, i_pre, f_pre):
    b_, h_, s_, dh = q.shape
    nc = s_ // CHUNK
    q = q.reshape(b_, h_, nc, CHUNK, dh)
    k = k.reshape(b_, h_, nc, CHUNK, dh)
    v = v.reshape(b_, h_, nc, CHUNK, dh)
    ig = i_pre.reshape(b_, h_, nc, CHUNK)
    bcum = jnp.cumsum(jax.nn.log_sigmoid(f_pre).reshape(b_, h_, nc, CHUNK), axis=-1)
    b_end = bcum[..., -1]

    a = b_end[..., None] - bcum + ig
    a_max = jnp.max(a, axis=-1)
    wa = jnp.exp(a - a_max[..., None])
    d_c = jnp.einsum('bhcsv,bhcsk->bhcvk', v * wa[..., None], k)
    d_n = jnp.einsum('bhcs,bhcsk->bhck', wa, k)

    def step(carry, xs):
        c_st, n_st, m_st = carry
        be, am, dc, dn = xs
        m_new = jnp.maximum(be + m_st, am)
        decay = jnp.exp(be + m_st - m_new)
        inw = jnp.exp(am - m_new)
        c_new = decay[..., None, None] * c_st + inw[..., None, None] * dc
        n_new = decay[..., None] * n_st + inw[..., None] * dn
        return (c_new, n_new, m_new), (c_st, n_st, m_st)

    init = (jnp.zeros((b_, h_, dh, dh), jnp.float32),
            jnp.zeros((b_, h_, dh), jnp.float32),
            jnp.zeros((b_, h_), jnp.float32))
    xs = (jnp.moveaxis(b_end, 2, 0), jnp.moveaxis(a_max, 2, 0),
          jnp.moveaxis(d_c, 2, 0), jnp.moveaxis(d_n, 2, 0))
    _, (c_prev, n_prev, m_prev) = lax.scan(step, init, xs)
    c_prev = jnp.moveaxis(c_prev, 0, 2)
    n_prev = jnp.moveaxis(n_prev, 0, 2)
    m_prev = jnp.moveaxis(m_prev, 0, 2)

    tri = jnp.tril(jnp.ones((CHUNK, CHUNK), dtype=bool))
    log_d = jnp.where(tri, bcum[..., :, None] - bcum[..., None, :] + ig[..., None, :], -jnp.inf)
    log_inter = bcum + m_prev[..., None]
    m_t = jnp.maximum(log_inter, jnp.max(log_d, axis=-1))
    w_inter = jnp.exp(log_inter - m_t)
    w_intra = jnp.exp(log_d - m_t[..., None]) * jnp.einsum('bhctd,bhcsd->bhcts', q, k)
    num = (w_inter[..., None] * jnp.einsum('bhcvk,bhctk->bhctv', c_prev, q)
           + jnp.einsum('bhcts,bhcsv->bhctv', w_intra, v))
    den = w_inter * jnp.einsum('bhck,bhctk->bhct', n_prev, q) + jnp.sum(w_intra, axis=-1)
    h = num / jnp.maximum(jnp.abs(den), jnp.exp(-m_t))[..., None]
    return h.reshape(b_, h_, s_, dh)


def _mlstm_branch(mq, mk, mv, mi, mf, mo, mz, gate_b, conv_w, conv_b, norm_g):
    b_, s_, _ = mq.shape
    dt = mq.dtype
    qk = jax.nn.silu(_causal_conv(jnp.concatenate([mq, mk], axis=-1), conv_w, conv_b))
    q, k = jnp.split(qk, 2, axis=-1)

    def heads(t):
        return t.astype(jnp.float32).reshape(b_, s_, M_HEADS, M_HEAD_DIM).transpose(0, 2, 1, 3)

    i_pre = (mi + gate_b[:M_HEADS]).astype(jnp.float32).transpose(0, 2, 1)
    f_pre = (mf + gate_b[M_HEADS:]).astype(jnp.float32).transpose(0, 2, 1)
    h = _mlstm_chunkwise(heads(q), heads(k) * (M_HEAD_DIM ** -0.5), heads(mv), i_pre, f_pre)
    h = h.transpose(0, 2, 1, 3) * jax.nn.sigmoid(mo.astype(jnp.float32)).reshape(
        b_, s_, M_HEADS, M_HEAD_DIM)
    h = h * lax.rsqrt(jnp.mean(h * h, axis=-1, keepdims=True) + EPS)
    h = h.reshape(b_, s_, M_WIDTH) * norm_g.astype(jnp.float32)
    return (h * jax.nn.silu(mz.astype(jnp.float32))).astype(dt)


def _pool_branch(pu, pz, pool_w, pool_scale):
    b_, s_, _ = pu.shape
    dt = pu.dtype
    u = pu.astype(jnp.float32).reshape(b_, s_, P_GROUPS, P_GROUP_DIM)
    cs = jnp.concatenate([jnp.zeros((b_, 1, P_GROUPS, P_GROUP_DIM), jnp.float32),
                          jnp.cumsum(u, axis=1)], axis=1)
    t1 = jnp.arange(1, s_ + 1)[:, None]
    lo = jnp.maximum(t1 - jnp.array(POOL_WINDOWS)[None, :], 0)
    idx = jnp.broadcast_to(lo[None, :, :, None], (b_, s_, P_GROUPS, P_GROUP_DIM))
    cs_lo = jnp.take_along_axis(cs, idx, axis=1)
    count = (t1 - lo).astype(jnp.float32)[None, :, :, None]
    p = (cs[:, 1:] - cs_lo) / count - u
    p = jnp.einsum('bsgc,gcd->bsgd', p, pool_w.astype(jnp.float32))
    p = p.reshape(b_, s_, P_WIDTH) * pool_scale.astype(jnp.float32)
    return (p * jax.nn.silu(pz.astype(jnp.float32))).astype(dt)


def _stick_breaking_branch(sq, sk, sv, sz):
    b_, s_, _ = sq.shape
    dt = sq.dtype

    def heads(t):
        return t.astype(jnp.float32).reshape(b_, s_, S_HEADS, S_HEAD_DIM).transpose(0, 2, 1, 3)

    q = heads(sq) * (S_HEAD_DIM ** -0.5)
    k = heads(sk)
    v = heads(sv)
    outs = []
    for blk in range(s_ // SB_BLOCK):
        q0 = blk * SB_BLOCK
        kend = q0 + SB_BLOCK
        z = jnp.einsum('bhtd,bhsd->bhts', q[:, :, q0:kend], k[:, :, :kend])
        t_pos = q0 + jnp.arange(SB_BLOCK)
        s_pos = jnp.arange(kend)
        causal = s_pos[None, :] < t_pos[:, None]
        log_fail = jnp.where(causal, jax.nn.log_sigmoid(-z), 0.0)
        log_surv = lax.cumsum(log_fail, axis=3, reverse=True) - log_fail
        attn = jnp.where(causal, jnp.exp(jax.nn.log_sigmoid(z) + log_surv), 0.0)
        outs.append(jnp.einsum('bhts,bhsd->bhtd', attn, v[:, :, :kend]))
    o = jnp.concatenate(outs, axis=2).transpose(0, 2, 1, 3).reshape(b_, s_, S_WIDTH)
    return (o * jax.nn.silu(sz.astype(jnp.float32))).astype(dt)


def setup_inputs(seed: int = 0) -> dict:
    key = jax.random.key(seed)
    ks = jax.random.split(key, 20)
    f32 = jnp.float32
    nrm = lambda k, shape: jax.random.normal(k, shape, f32)
    i_bias = 0.1 * nrm(ks[6], (DEPTH, M_HEADS))
    f_bias = jnp.linspace(3.0, 6.0, M_HEADS, dtype=f32)[None, :] + 0.1 * nrm(ks[7], (DEPTH, M_HEADS))
    return {
        "x": nrm(ks[0], (BATCH, SEQ, D_MODEL)),
        "c": nrm(ks[1], (BATCH, D_MODEL)),
        "norm_g": 1.0 + 0.05 * nrm(ks[2], (DEPTH, D_MODEL)),
        "w_ada": nrm(ks[3], (DEPTH, D_MODEL, 3 * D_MODEL)) * (0.5 * D_MODEL ** -0.5),
        "b_ada": 0.02 * nrm(ks[4], (DEPTH, 3 * D_MODEL)),
        "w_in": nrm(ks[5], (DEPTH, D_MODEL, N_IN)) * D_MODEL ** -0.5,
        "m_gate_b": jnp.concatenate([i_bias, f_bias], axis=-1),
        "conv_w": nrm(ks[8], (DEPTH, CONV_K, 2 * M_WIDTH)) * CONV_K ** -0.5,
        "conv_b": 0.02 * nrm(ks[9], (DEPTH, 2 * M_WIDTH)),
        "m_norm_g": 1.0 + 0.05 * nrm(ks[10], (DEPTH, M_WIDTH)),
        "pool_w": nrm(ks[11], (DEPTH, P_GROUPS, P_GROUP_DIM, P_GROUP_DIM)) * P_GROUP_DIM ** -0.5,
        "pool_scale": 1.0 + 0.05 * nrm(ks[12], (DEPTH, P_WIDTH)),
        "w_br_m": nrm(ks[13], (DEPTH, M_WIDTH, D_MODEL)) * M_WIDTH ** -0.5,
        "w_br_p": nrm(ks[14], (DEPTH, P_WIDTH, D_MODEL)) * P_WIDTH ** -0.5,
        "w_br_s": nrm(ks[15], (DEPTH, S_WIDTH, D_MODEL)) * S_WIDTH ** -0.5,
        "gate_b": 0.02 * nrm(ks[16], (DEPTH, N_BRANCH * D_MODEL)),
        "w_out": nrm(ks[17], (DEPTH, D_MODEL, D_MODEL)) * D_MODEL ** -0.5,
        "final_g": 1.0 + 0.05 * nrm(ks[18], (D_MODEL,)),
    }


def reference(x, c, norm_g, w_ada, b_ada, w_in, m_gate_b, conv_w, conv_b, m_norm_g,
              pool_w, pool_scale, w_br_m, w_br_p, w_br_s, gate_b, w_out, final_g):
    b_, s_, _ = x.shape
    for l in range(DEPTH):
        mod = c @ w_ada[l] + b_ada[l]
        shift, scale, gate = jnp.split(mod, 3, axis=-1)
        h = _rmsnorm(x, norm_g[l]) * (1.0 + scale[:, None, :]) + shift[:, None, :]
        proj = h @ w_in[l]
        (mq, mk, mv, mi, mf, mo, mz, pu, pz, sq, sk, sv, sz, gpre) = jnp.split(
            proj, _split_points(), axis=-1)
        y_m = _mlstm_branch(mq, mk, mv, mi, mf, mo, mz, m_gate_b[l], conv_w[l], conv_b[l], m_norm_g[l])
        y_p = _pool_branch(pu, pz, pool_w[l], pool_scale[l])
        y_s = _stick_breaking_branch(sq, sk, sv, sz)
        g = jax.nn.sigmoid((gpre + gate_b[l]).astype(jnp.float32)).astype(x.dtype)
        g = g.reshape(b_, s_, N_BRANCH, D_MODEL)
        merged = (g[:, :, 0] * (y_m @ w_br_m[l])
                  + g[:, :, 1] * (y_p @ w_br_p[l])
                  + g[:, :, 2] * (y_s @ w_br_s[l]))
        x = x + gate[:, None, :] * (merged @ w_out[l])
    return _rmsnorm(x, final_g)
```

```python
import functools

import jax
import jax.numpy as jnp
from jax import lax
from jax.experimental import pallas as pl
from jax.experimental.pallas import tpu as pltpu

F32 = jnp.float32
BF16 = jnp.bfloat16

EPS = 1e-6
CHUNK = 64
M_HEADS = 4
M_HEAD_DIM = 128
M_WIDTH = M_HEADS * M_HEAD_DIM
CONV_K = 4
P_GROUPS = 4
P_GROUP_DIM = 64
P_WIDTH = P_GROUPS * P_GROUP_DIM
POOL_WINDOWS = (2, 4, 8, 16)
S_HEADS = 4
S_HEAD_DIM = 64
S_WIDTH = S_HEADS * S_HEAD_DIM
N_BRANCH = 3

V7X_LANES = 128
V7X_SUBLANES = 8
V7X_VMEM_LIMIT_BYTES = 56 * 1024 * 1024

ROW_TILE = 512
SB_TILE = 128
POOL_HALO = 16
POOL_PAD = 8

SB_DEAD_LOG = -120.0


def _sigmoid(x):
    return 1.0 / (1.0 + jnp.exp(-x))


def _silu(x):
    return x * _sigmoid(x)


def _log_sigmoid(x):
    return jnp.minimum(x, 0.0) - jnp.log1p(jnp.exp(-jnp.abs(x)))


def _split_hi_lo(a):
    hi = a.astype(BF16)
    lo = (a - hi.astype(F32)).astype(BF16)
    return hi, lo


def _dot(a, b):
    return jnp.dot(a, b, preferred_element_type=F32)


def _dot_nt(a, b):
    return lax.dot_general(a, b, (((1,), (1,)), ((), ())), preferred_element_type=F32)


def _dot_tn(a, b):
    return lax.dot_general(a, b, (((0,), (0,)), ((), ())), preferred_element_type=F32)


def _ada_kernel(c_ref, w_ref, b_ref, o_ref):
    o_ref[0] = jnp.dot(c_ref[...], w_ref[0], preferred_element_type=F32,
                       precision=lax.Precision.HIGHEST) + b_ref[0]


def _ada_call(c_pad, w_ada, b_ada):
    depth, d, n3 = w_ada.shape
    tn = 1024
    return pl.pallas_call(
        _ada_kernel,
        grid=(depth, n3 // tn),
        in_specs=[
            pl.BlockSpec((c_pad.shape[0], d), lambda l, j: (0, 0)),
            pl.BlockSpec((1, d, tn), lambda l, j: (l, 0, j)),
            pl.BlockSpec((1, 1, tn), lambda l, j: (l, 0, j)),
        ],
        out_specs=pl.BlockSpec((1, c_pad.shape[0], tn), lambda l, j: (l, 0, j)),
        out_shape=jax.ShapeDtypeStruct((depth, c_pad.shape[0], n3), F32),
        compiler_params=pltpu.CompilerParams(
            dimension_semantics=("arbitrary", "arbitrary")),
        name="ada_mod",
    )(c_pad, w_ada, b_ada.reshape(depth, 1, n3))


def _proj_kernel(x_ref, mod_ref, ng_ref, wm_ref, wp_ref, ws_ref, wg_ref, wgt_ref, wgc_ref,
                 gbc_ref, gbr_ref,
                 om_ref, op_ref, os_ref, og_ref, ogr_ref, ogc_ref,
                 hb_ref):
    tm, d = x_ref.shape
    rc = 64
    g = ng_ref[...]
    scale1 = 1.0 + mod_ref[0, :, d:2 * d]
    shift = mod_ref[0, :, 0:d]

    def norm_rows(r, carry):
        r0 = pl.multiple_of(r * rc, rc)
        xf = x_ref[pl.ds(r0, rc), :]
        y = xf * lax.rsqrt(jnp.mean(xf * xf, axis=-1, keepdims=True) + EPS)
        y = y * g
        hb_ref[pl.ds(r0, rc), :] = (y * scale1 + shift).astype(BF16)
        return carry

    lax.fori_loop(0, tm // rc, norm_rows, 0)

    nch = 512
    for w_ref, o_ref in ((wm_ref, om_ref), (wp_ref, op_ref), (ws_ref, os_ref), (wg_ref, og_ref)):
        for c0 in range(0, w_ref.shape[1], nch):
            o_ref[:, c0:c0 + nch] = _dot(hb_ref[...], w_ref[:, c0:c0 + nch]).astype(o_ref.dtype)

    ti = lax.broadcasted_iota(jnp.int32, (tm, tm), 0)
    tj = lax.broadcasted_iota(jnp.int32, (tm, tm), 1)
    chunk_shift = CHUNK.bit_length() - 1
    same_chunk = jnp.right_shift(ti, chunk_shift) == jnp.right_shift(tj, chunk_shift)
    upper = jnp.where(same_chunk & (ti <= tj), 1.0, 0.0).astype(BF16)
    lower = jnp.where(same_chunk & (tj <= ti), 1.0, 0.0).astype(BF16)

    gr = _dot_nt(wgt_ref[...], hb_ref[...]) + gbc_ref[...]
    is_i_row = lax.broadcasted_iota(jnp.int32, gr.shape, 0) < M_HEADS
    vr = jnp.where(is_i_row, gr, _log_sigmoid(gr))
    hi, lo = _split_hi_lo(vr)
    csr = _dot(hi, upper) + _dot(lo, upper)
    outr = jnp.where(is_i_row, vr, csr)
    for ci in range(tm // CHUNK):
        ogr_ref[ci] = outr[:, ci * CHUNK:(ci + 1) * CHUNK]

    gc = _dot(hb_ref[...], wgc_ref[...]) + gbr_ref[...]
    is_i_col = lax.broadcasted_iota(jnp.int32, gc.shape, 1) < M_HEADS
    vc = jnp.where(is_i_col, gc, _log_sigmoid(gc))
    hi, lo = _split_hi_lo(vc)
    csc = _dot(lower, hi) + _dot(lower, lo)
    ogc_ref[...] = jnp.where(is_i_col, vc, csc)


def _proj_call(xf, mod_l, ng, wm, wp, ws, wg, wgt, wgc, gbc, gbr, *, batch):
    n, d = xf.shape
    tm = ROW_TILE
    tiles_per_b = n // batch // tm
    const = lambda i: (0, 0)
    resident = functools.partial(pl.BlockSpec, pipeline_mode=pl.Buffered(1))
    out_shapes = (
        jax.ShapeDtypeStruct((n, wm.shape[1]), BF16),
        jax.ShapeDtypeStruct((n, wp.shape[1]), BF16),
        jax.ShapeDtypeStruct((n, ws.shape[1]), BF16),
        jax.ShapeDtypeStruct((n, wg.shape[1]), BF16),
        jax.ShapeDtypeStruct((n // CHUNK, 2 * M_HEADS, CHUNK), F32),
        jax.ShapeDtypeStruct((n, V7X_LANES), F32),
    )
    return pl.pallas_call(
        _proj_kernel,
        grid=(n // tm,),
        in_specs=[
            pl.BlockSpec((tm, d), lambda i: (i, 0)),
            pl.BlockSpec((1, 1, 3 * d), lambda i: (i // tiles_per_b, 0, 0)),
            pl.BlockSpec((1, d), const),
            resident(wm.shape, const),
            resident(wp.shape, const),
            resident(ws.shape, const),
            resident(wg.shape, const),
            resident(wgt.shape, const),
            resident(wgc.shape, const),
            pl.BlockSpec(gbc.shape, const),
            pl.BlockSpec(gbr.shape, const),
        ],
        out_specs=(
            pl.BlockSpec((tm, wm.shape[1]), lambda i: (i, 0)),
            pl.BlockSpec((tm, wp.shape[1]), lambda i: (i, 0)),
            pl.BlockSpec((tm, ws.shape[1]), lambda i: (i, 0)),
            pl.BlockSpec((tm, wg.shape[1]), lambda i: (i, 0)),
            pl.BlockSpec((tm // CHUNK, 2 * M_HEADS, CHUNK), lambda i: (i, 0, 0)),
            pl.BlockSpec((tm, V7X_LANES), lambda i: (i, 0)),
        ),
        out_shape=out_shapes,
        scratch_shapes=[pltpu.VMEM((tm, d), BF16)],
        compiler_params=pltpu.CompilerParams(
            dimension_semantics=("arbitrary",), vmem_limit_bytes=V7X_VMEM_LIMIT_BYTES),
        name="norm_proj",
    )(xf, mod_l, ng, wm, wp, ws, wg, wgt, wgc, gbc, gbr)


def _mlstm_kernel(m_ref, gr_ref, gc_ref, cw_ref, cb_ref, ng_ref, o_ref,
                  xbuf, q_s, k_s, vx_s, st_s, m_s, hbuf):
    t = m_ref.shape[0]
    j = pl.program_id(1)
    qk_w = 2 * M_WIDTH
    hd = M_HEAD_DIM

    @pl.when(j == 0)
    def _():
        xbuf[0:V7X_SUBLANES, :] = jnp.zeros((V7X_SUBLANES, qk_w), F32)
        st_s[...] = jnp.zeros_like(st_s)
        m_s[...] = jnp.zeros_like(m_s)
        ones_col = jnp.where(lax.broadcasted_iota(jnp.int32, (t, hd), 1) == 0, 1.0, 0.0).astype(BF16)
        for h in range(M_HEADS):
            vx_s[h, :, hd:2 * hd] = ones_col

    @pl.when(j != 0)
    def _():
        xbuf[0:V7X_SUBLANES, :] = xbuf[t:t + V7X_SUBLANES, :]

    rows = 128
    for r0 in range(0, t, rows):
        xbuf[V7X_SUBLANES + r0:V7X_SUBLANES + r0 + rows, :] = m_ref[r0:r0 + rows, 0:qk_w].astype(F32)
    for r0 in range(0, t, rows):
        acc = cb_ref[...] + cw_ref[0:1, :] * xbuf[r0 + 5:r0 + 5 + rows, :]
        for tap in range(1, CONV_K):
            acc = acc + cw_ref[tap:tap + 1, :] * xbuf[r0 + 5 + tap:r0 + 5 + tap + rows, :]
        act = _silu(acc)
        q_s[r0:r0 + rows, :] = act[:, 0:M_WIDTH].astype(BF16)
        k_s[r0:r0 + rows, :] = (act[:, M_WIDTH:qk_w] * (hd ** -0.5)).astype(BF16)
        for h in range(M_HEADS):
            vx_s[h, r0:r0 + rows, 0:hd] = m_ref[r0:r0 + rows, qk_w + h * hd:qk_w + (h + 1) * hd]

    ti = lax.broadcasted_iota(jnp.int32, (CHUNK, CHUNK), 0)
    si = lax.broadcasted_iota(jnp.int32, (CHUNK, CHUNK), 1)
    tri = si <= ti

    def chunk_body(c, carry):
        r0 = pl.multiple_of(c * CHUNK, CHUNK)
        grow = gr_ref[c]
        ig_r = grow[0:M_HEADS, :]
        bc_r = grow[M_HEADS:2 * M_HEADS, :]
        b_end = bc_r[:, CHUNK - 1:CHUNK]
        a_r = b_end - bc_r + ig_r
        a_max = jnp.max(a_r, axis=1, keepdims=True)
        m_prev = m_s[0:M_HEADS, 0:1]
        m_new = jnp.maximum(b_end + m_prev, a_max)
        decay = jnp.exp(b_end + m_prev - m_new)
        inw = jnp.exp(a_max - m_new)
        m_s[0:M_HEADS, 0:1] = m_new
        rr = ig_r - bc_r
        gcol = gc_ref[pl.ds(r0, CHUNK), :]
        for h in range(M_HEADS):
            ig_c = gcol[:, h:h + 1]
            bc_c = gcol[:, M_HEADS + h:M_HEADS + h + 1]
            mp = m_prev[h:h + 1, :]
            q_c = q_s[pl.ds(r0, CHUNK), h * hd:(h + 1) * hd]
            k_c = k_s[pl.ds(r0, CHUNK), h * hd:(h + 1) * hd]
            vx_c = vx_s[h, pl.ds(r0, CHUNK), :]
            log_d = jnp.where(tri, bc_c + rr[h:h + 1, :], -jnp.inf)
            log_inter = bc_c + mp
            m_t = jnp.maximum(log_inter, jnp.max(log_d, axis=1, keepdims=True))
            w_inter = jnp.exp(log_inter - m_t)
            w_intra = jnp.exp(log_d - m_t) * _dot_nt(q_c, k_c)
            inter = _dot(q_c, st_s[h].astype(BF16))
            intra = _dot(w_intra.astype(BF16), vx_c)
            hx = w_inter * inter + intra
            den = hx[:, hd:hd + 1]
            hbuf[pl.ds(r0, CHUNK), h * hd:(h + 1) * hd] = (
                hx[:, 0:hd] / jnp.maximum(jnp.abs(den), jnp.exp(-m_t)))
            wa_c = jnp.exp(b_end[h:h + 1, :] - bc_c + ig_c - a_max[h:h + 1, :])
            vw = (vx_c.astype(F32) * wa_c).astype(BF16)
            st_s[h] = decay[h:h + 1, :] * st_s[h] + inw[h:h + 1, :] * _dot_tn(k_c, vw)
        return carry

    lax.fori_loop(0, t // CHUNK, chunk_body, 0)

    o_off = qk_w + M_WIDTH
    z_off = o_off + M_WIDTH
    for r0 in range(0, t, rows):
        hg = hbuf[r0:r0 + rows, :] * _sigmoid(m_ref[r0:r0 + rows, o_off:o_off + M_WIDTH].astype(F32))
        zg = _silu(m_ref[r0:r0 + rows, z_off:z_off + M_WIDTH].astype(F32))
        for h in range(M_HEADS):
            hh = hg[:, h * hd:(h + 1) * hd]
            hn = hh * lax.rsqrt(jnp.mean(hh * hh, axis=-1, keepdims=True) + EPS)
            o_ref[r0:r0 + rows, h * hd:(h + 1) * hd] = (
                hn * ng_ref[:, h * hd:(h + 1) * hd] * zg[:, h * hd:(h + 1) * hd]).astype(o_ref.dtype)


def _mlstm_call(out_m, out_gr, out_gc, conv_w, conv_b, m_norm_g, *, batch):
    n, wm = out_m.shape
    t = ROW_TILE
    tiles_per_b = n // batch // t
    const = lambda b, j: (0, 0)
    return pl.pallas_call(
        _mlstm_kernel,
        grid=(batch, tiles_per_b),
        in_specs=[
            pl.BlockSpec((t, wm), lambda b, j: (b * tiles_per_b + j, 0)),
            pl.BlockSpec((t // CHUNK, 2 * M_HEADS, CHUNK), lambda b, j: (b * tiles_per_b + j, 0, 0)),
            pl.BlockSpec((t, V7X_LANES), lambda b, j: (b * tiles_per_b + j, 0)),
            pl.BlockSpec(conv_w.shape, const),
            pl.BlockSpec(conv_b.shape, const),
            pl.BlockSpec(m_norm_g.shape, const),
        ],
        out_specs=pl.BlockSpec((t, M_WIDTH), lambda b, j: (b * tiles_per_b + j, 0)),
        out_shape=jax.ShapeDtypeStruct((n, M_WIDTH), BF16),
        scratch_shapes=[
            pltpu.VMEM((t + 2 * V7X_SUBLANES, 2 * M_WIDTH), F32),
            pltpu.VMEM((t, M_WIDTH), BF16),
            pltpu.VMEM((t, M_WIDTH), BF16),
            pltpu.VMEM((M_HEADS, t, 2 * M_HEAD_DIM), BF16),
            pltpu.VMEM((M_HEADS, M_HEAD_DIM, 2 * M_HEAD_DIM), F32),
            pltpu.VMEM((V7X_SUBLANES, V7X_LANES), F32),
            pltpu.VMEM((t, M_WIDTH), F32),
        ],
        compiler_params=pltpu.CompilerParams(
            dimension_semantics=("arbitrary", "arbitrary"), vmem_limit_bytes=V7X_VMEM_LIMIT_BYTES),
        name="mlstm",
    )(out_m, out_gr, out_gc, conv_w, conv_b, m_norm_g)


def _sb_kernel(q_ref, k_ref, v_ref, z_ref, o_ref, qm_s, carry_s, acc_s, *, dead_log):
    tq = q_ref.shape[0]
    tk = SB_TILE
    i = pl.program_id(1)
    lane_head = jnp.right_shift(lax.broadcasted_iota(jnp.int32, (1, S_WIDTH), 1),
                                S_HEAD_DIM.bit_length() - 1)

    q = q_ref[...] * (S_HEAD_DIM ** -0.5)
    for h in range(S_HEADS):
        qm_s[h] = jnp.where(lane_head == h, q, jnp.zeros_like(q))
    carry_s[...] = jnp.zeros_like(carry_s)
    acc_s[...] = jnp.zeros_like(acc_s)

    uj = lax.broadcasted_iota(jnp.int32, (tk, 2 * tk), 0)
    us = lax.broadcasted_iota(jnp.int32, (tk, 2 * tk), 1)
    after = jnp.where((uj > us) | (us >= tk), 1.0, 0.0).astype(BF16)
    causal = (lax.broadcasted_iota(jnp.int32, (tq, tk), 1)
              < lax.broadcasted_iota(jnp.int32, (tq, tk), 0))

    def tile(kt, diag):
        k0 = pl.multiple_of(kt * tk, tk)
        kk = k_ref[pl.ds(k0, tk), :]
        vv = v_ref[pl.ds(k0, tk), :]
        acc = acc_s[...]
        worst = None
        for h in range(S_HEADS):
            z = _dot_nt(qm_s[h], kk)
            l1p = jnp.log1p(jnp.exp(-jnp.abs(z)))
            log_fail = -(jnp.maximum(z, 0.0) + l1p)
            log_beta = jnp.minimum(z, 0.0) - l1p
            if diag:
                log_fail = jnp.where(causal, log_fail, 0.0)
            hi, lo = _split_hi_lo(log_fail)
            cum = _dot(hi, after) + _dot(lo, after)
            attn = jnp.exp(log_beta + cum[:, 0:tk] + carry_s[h])
            if diag:
                attn = jnp.where(causal, attn, 0.0)
            pv = _dot(attn.astype(BF16), vv)
            acc = acc + jnp.where(lane_head == h, pv, 0.0)
            new_carry = carry_s[h] + cum[:, tk:2 * tk]
            carry_s[h] = new_carry
            worst = new_carry if worst is None else jnp.maximum(worst, new_carry)
        acc_s[...] = acc
        return jnp.max(worst)

    alive = tile(i, True)

    def cond(state):
        kt, alive = state
        return (kt >= 0) & (alive > dead_log)

    def body(state):
        kt, _ = state
        return kt - 1, tile(kt, False)

    lax.while_loop(cond, body, (i - 1, alive))

    o_ref[...] = (acc_s[...] * _silu(z_ref[...].astype(F32))).astype(o_ref.dtype)


def _sb_call(out_s, *, batch, dead_log):
    n, _ = out_s.shape
    s = n // batch
    tq = SB_TILE
    nq = s // tq
    return pl.pallas_call(
        functools.partial(_sb_kernel, dead_log=dead_log),
        grid=(batch, nq),
        in_specs=[
            pl.BlockSpec((tq, S_WIDTH), lambda b, i: (b * nq + i, 0)),
            pl.BlockSpec((s, S_WIDTH), lambda b, i: (b, 1)),
            pl.BlockSpec((s, S_WIDTH), lambda b, i: (b, 2)),
            pl.BlockSpec((tq, S_WIDTH), lambda b, i: (b * nq + i, 3)),
        ],
        out_specs=pl.BlockSpec((tq, S_WIDTH), lambda b, i: (b * nq + i, 0)),
        out_shape=jax.ShapeDtypeStruct((n, S_WIDTH), BF16),
        scratch_shapes=[
            pltpu.VMEM((S_HEADS, tq, S_WIDTH), BF16),
            pltpu.VMEM((S_HEADS, tq, SB_TILE), F32),
            pltpu.VMEM((tq, S_WIDTH), F32),
        ],
        compiler_params=pltpu.CompilerParams(
            dimension_semantics=("arbitrary", "arbitrary"), vmem_limit_bytes=V7X_VMEM_LIMIT_BYTES),
        name="stick_breaking",
    )(out_s, out_s, out_s, out_s)


def _merge_kernel(x_ref, mod_ref, ym_ref, ys_ref, p_ref, pprev_ref, g_ref, gb_ref,
                  wm_ref, wp_ref, ws_ref, wo_ref, pw_ref, ps_ref, fg_ref,
                  o_ref, pa_s, pb_s, yp_s, *, tiles_per_b, final):
    tm, d = x_ref.shape
    i = pl.program_id(0)
    jb = i % tiles_per_b
    top = POOL_PAD + POOL_HALO

    pa_s[0:POOL_PAD, :] = jnp.zeros((POOL_PAD, P_WIDTH), F32)
    pb_s[0:POOL_PAD, :] = jnp.zeros((POOL_PAD, P_WIDTH), F32)
    halo = pprev_ref[:, 0:P_WIDTH].astype(F32)
    pa_s[POOL_PAD:top, :] = jnp.where(jb == 0, jnp.zeros_like(halo), halo)
    u = p_ref[:, 0:P_WIDTH].astype(F32)
    pa_s[top:top + tm, :] = u
    nrow = POOL_HALO + tm
    lane_group = jnp.right_shift(lax.broadcasted_iota(jnp.int32, (1, P_WIDTH), 1),
                                 P_GROUP_DIM.bit_length() - 1)
    src, dst = pa_s, pb_s
    wsum = None
    for level, shift in enumerate((1, 2, 4, 8)):
        summed = src[POOL_PAD:POOL_PAD + nrow, :] + src[POOL_PAD - shift:POOL_PAD - shift + nrow, :]
        dst[POOL_PAD:POOL_PAD + nrow, :] = summed
        cur = dst[top:top + tm, :]
        wsum = cur if wsum is None else jnp.where(lane_group >= level, cur, wsum)
        src, dst = dst, src
    window = jnp.left_shift(2, lane_group).astype(F32)
    tpos = (jb * tm + lax.broadcasted_iota(jnp.int32, (tm, 1), 0) + 1).astype(F32)
    count = jnp.minimum(tpos, window)
    pooled = wsum / count - u
    pm = _dot(pooled.astype(BF16), pw_ref[...])
    yp_s[...] = (pm * ps_ref[...] * _silu(p_ref[:, P_WIDTH:2 * P_WIDTH].astype(F32))).astype(BF16)

    gate = mod_ref[0, :, 2 * d:3 * d]
    rows = 256
    for r0 in range(0, tm, rows):
        rs = slice(r0, r0 + rows)
        merged = (_sigmoid(g_ref[rs, 0:d].astype(F32) + gb_ref[:, 0:d]) * _dot(ym_ref[rs, :], wm_ref[...])
                  + _sigmoid(g_ref[rs, d:2 * d].astype(F32) + gb_ref[:, d:2 * d]) * _dot(yp_s[rs, :], wp_ref[...])
                  + _sigmoid(g_ref[rs, 2 * d:3 * d].astype(F32) + gb_ref[:, 2 * d:3 * d]) * _dot(ys_ref[rs, :], ws_ref[...]))
        xn = x_ref[rs, :] + gate * _dot(merged.astype(BF16), wo_ref[...])
        if final:
            xn = xn * lax.rsqrt(jnp.mean(xn * xn, axis=-1, keepdims=True) + EPS) * fg_ref[...]
        o_ref[rs, :] = xn


def _merge_call(xf, mod_l, y_m, y_s, out_p, out_g, gate_b, wbm, wbp, wbs, wo, pw_bd, pscale, final_g,
                *, batch, final):
    n, d = xf.shape
    tm = ROW_TILE
    tiles_per_b = n // batch // tm
    halo_blocks = tm // POOL_HALO
    const = lambda i: (0, 0)
    resident = functools.partial(pl.BlockSpec, pipeline_mode=pl.Buffered(1))
    return pl.pallas_call(
        functools.partial(_merge_kernel, tiles_per_b=tiles_per_b, final=final),
        grid=(n // tm,),
        in_specs=[
            pl.BlockSpec((tm, d), lambda i: (i, 0)),
            pl.BlockSpec((1, 1, 3 * d), lambda i: (i // tiles_per_b, 0, 0)),
            pl.BlockSpec((tm, M_WIDTH), lambda i: (i, 0)),
            pl.BlockSpec((tm, S_WIDTH), lambda i: (i, 0)),
            pl.BlockSpec((tm, 2 * P_WIDTH), lambda i: (i, 0)),
            pl.BlockSpec((POOL_HALO, 2 * P_WIDTH), lambda i: (jnp.maximum(i * halo_blocks - 1, 0), 0)),
            pl.BlockSpec((tm, N_BRANCH * d), lambda i: (i, 0)),
            pl.BlockSpec((1, N_BRANCH * d), const),
            resident(wbm.shape, const),
            resident(wbp.shape, const),
            resident(wbs.shape, const),
            resident(wo.shape, const),
            resident(pw_bd.shape, const),
            pl.BlockSpec((1, P_WIDTH), const),
            pl.BlockSpec((1, d), const),
        ],
        out_specs=pl.BlockSpec((tm, d), lambda i: (i, 0)),
        out_shape=jax.ShapeDtypeStruct((n, d), F32),
        scratch_shapes=[
            pltpu.VMEM((POOL_PAD + POOL_HALO + tm, P_WIDTH), F32),
            pltpu.VMEM((POOL_PAD + POOL_HALO + tm, P_WIDTH), F32),
            pltpu.VMEM((tm, P_WIDTH), BF16),
        ],
        compiler_params=pltpu.CompilerParams(
            dimension_semantics=("arbitrary",), vmem_limit_bytes=V7X_VMEM_LIMIT_BYTES),
        name="pool_merge",
    )(xf, mod_l, y_m, y_s, out_p, out_p, out_g, gate_b, wbm, wbp, wbs, wo, pw_bd, pscale, final_g)


def _split_w_in(w):
    o = 0
    sizes = dict(mq=M_WIDTH, mk=M_WIDTH, mv=M_WIDTH, mi=M_HEADS, mf=M_HEADS, mo=M_WIDTH, mz=M_WIDTH,
                 pu=P_WIDTH, pz=P_WIDTH, sq=S_WIDTH, sk=S_WIDTH, sv=S_WIDTH, sz=S_WIDTH)
    cols = {}
    for name, sz in sizes.items():
        cols[name] = w[:, o:o + sz]
        o += sz
    cols["gpre"] = w[:, o:]
    return cols


def kernel(x, c, norm_g, w_ada, b_ada, w_in, m_gate_b, conv_w, conv_b, m_norm_g, pool_w, pool_scale,
           w_br_m, w_br_p, w_br_s, gate_b, w_out, final_g):
    batch, seq, d = x.shape
    depth = w_in.shape[0]
    n = batch * seq
    assert seq % ROW_TILE == 0 and seq % SB_TILE == 0 and d % V7X_LANES == 0

    c_pad = jnp.zeros((V7X_SUBLANES, d), F32).at[:batch].set(c)
    mod = _ada_call(c_pad, w_ada, b_ada)

    xf = x.reshape(n, d)
    for l in range(depth):
        mod_l = mod[l, :batch].reshape(batch, 1, 3 * d)
        cols = _split_w_in(w_in[l])
        wm = jnp.concatenate([cols[k] for k in ("mq", "mk", "mv", "mo", "mz")], axis=1).astype(BF16)
        wp = jnp.concatenate([cols["pu"], cols["pz"]], axis=1).astype(BF16)
        ws = jnp.concatenate([cols[k] for k in ("sq", "sk", "sv", "sz")], axis=1).astype(BF16)
        wg = cols["gpre"].astype(BF16)
        w_gate = jnp.concatenate([cols["mi"], cols["mf"]], axis=1)
        wgt = w_gate.T.astype(BF16)
        wgc = jnp.zeros((d, V7X_LANES), F32).at[:, :2 * M_HEADS].set(w_gate).astype(BF16)
        gbc = m_gate_b[l].reshape(2 * M_HEADS, 1)
        gbr = jnp.zeros((1, V7X_LANES), F32).at[0, :2 * M_HEADS].set(m_gate_b[l])

        out_m, out_p, out_s, out_g, out_gr, out_gc = _proj_call(
            xf, mod_l, norm_g[l].reshape(1, d), wm, wp, ws, wg, wgt, wgc, gbc, gbr, batch=batch)

        y_m = _mlstm_call(out_m, out_gr, out_gc, conv_w[l], conv_b[l].reshape(1, -1),
                          m_norm_g[l].reshape(1, -1), batch=batch)
        y_s = _sb_call(out_s, batch=batch, dead_log=SB_DEAD_LOG)

        pw_bd = jnp.zeros((P_WIDTH, P_WIDTH), F32)
        for g in range(P_GROUPS):
            sl = slice(g * P_GROUP_DIM, (g + 1) * P_GROUP_DIM)
            pw_bd = pw_bd.at[sl, sl].set(pool_w[l, g])
        xf = _merge_call(
            xf, mod_l, y_m, y_s, out_p, out_g, gate_b[l].reshape(1, -1),
            w_br_m[l].astype(BF16), w_br_p[l].astype(BF16), w_br_s[l].astype(BF16),
            w_out[l].astype(BF16), pw_bd.astype(BF16), pool_scale[l].reshape(1, -1),
            final_g.reshape(1, d), batch=batch, final=(l == depth - 1))
    return xf.reshape(batch, seq, d)
```

```python
import functools

import jax
import jax.numpy as jnp
from jax import lax
from jax.experimental import pallas as pl
from jax.experimental.pallas import tpu as pltpu

F32 = jnp.float32
BF16 = jnp.bfloat16

EPS = 1e-6
CHUNK = 256
M_HEADS = 4
M_HEAD_DIM = 128
M_WIDTH = M_HEADS * M_HEAD_DIM
CONV_K = 4
P_GROUPS = 4
P_GROUP_DIM = 64
P_WIDTH = P_GROUPS * P_GROUP_DIM
POOL_WINDOWS = (2, 4, 8, 16)
S_HEADS = 4
S_HEAD_DIM = 64
S_WIDTH = S_HEADS * S_HEAD_DIM
N_BRANCH = 3

V7X_LANES = 128
V7X_SUBLANES = 8
V7X_VMEM_LIMIT_BYTES = 56 * 1024 * 1024

ROW_TILE = 512
SB_TILE = 128
POOL_HALO = 16
POOL_PAD = 8

SB_DEAD_LOG = -120.0


def _sigmoid(x):
    return 1.0 / (1.0 + jnp.exp(-x))


def _silu(x):
    return x * _sigmoid(x)


def _log_sigmoid(x):
    return jnp.minimum(x, 0.0) - jnp.log1p(jnp.exp(-jnp.abs(x)))


def _split_hi_lo(a):
    hi = a.astype(BF16)
    lo = (a - hi.astype(F32)).astype(BF16)
    return hi, lo


def _dot(a, b):
    return jnp.dot(a, b, preferred_element_type=F32)


def _dot_nt(a, b):
    return lax.dot_general(a, b, (((1,), (1,)), ((), ())), preferred_element_type=F32)


def _dot_tn(a, b):
    return lax.dot_general(a, b, (((0,), (0,)), ((), ())), preferred_element_type=F32)


def _ada_kernel(c_ref, w_ref, b_ref, o_ref):
    o_ref[0] = jnp.dot(c_ref[...], w_ref[0], preferred_element_type=F32,
                       precision=lax.Precision.HIGHEST) + b_ref[0]


def _ada_call(c_pad, w_ada, b_ada):
    depth, d, n3 = w_ada.shape
    tn = 1024
    return pl.pallas_call(
        _ada_kernel,
        grid=(depth, n3 // tn),
        in_specs=[
            pl.BlockSpec((c_pad.shape[0], d), lambda l, j: (0, 0)),
            pl.BlockSpec((1, d, tn), lambda l, j: (l, 0, j)),
            pl.BlockSpec((1, 1, tn), lambda l, j: (l, 0, j)),
        ],
        out_specs=pl.BlockSpec((1, c_pad.shape[0], tn), lambda l, j: (l, 0, j)),
        out_shape=jax.ShapeDtypeStruct((depth, c_pad.shape[0], n3), F32),
        compiler_params=pltpu.CompilerParams(
            dimension_semantics=("arbitrary", "arbitrary")),
        name="ada_mod",
    )(c_pad, w_ada, b_ada.reshape(depth, 1, n3))


def _proj_kernel(x_ref, mod_ref, ng_ref, wm_ref, wp_ref, ws_ref, wg_ref, wgt_ref, wgc_ref,
                 gbc_ref, gbr_ref, mgb_ref,
                 om_ref, op_ref, os_ref, og_ref, ogr_ref, ogc_ref,
                 hb_ref):
    tm, d = x_ref.shape
    rc = 64
    g = ng_ref[...]
    scale1 = 1.0 + mod_ref[0, :, d:2 * d]
    shift = mod_ref[0, :, 0:d]

    def norm_rows(r, carry):
        r0 = pl.multiple_of(r * rc, rc)
        xf = x_ref[pl.ds(r0, rc), :]
        y = xf * lax.rsqrt(jnp.mean(xf * xf, axis=-1, keepdims=True) + EPS)
        y = y * g
        hb_ref[pl.ds(r0, rc), :] = (y * scale1 + shift).astype(BF16)
        return carry

    lax.fori_loop(0, tm // rc, norm_rows, 0)

    plan = [(wm_ref, om_ref, c0, M_WIDTH, None) for c0 in range(0, 3 * M_WIDTH, M_WIDTH)]
    plan += [(wm_ref, om_ref, 3 * M_WIDTH, M_WIDTH, _sigmoid),
             (wm_ref, om_ref, 4 * M_WIDTH, M_WIDTH, _silu),
             (wp_ref, op_ref, 0, P_WIDTH, None), (wp_ref, op_ref, P_WIDTH, P_WIDTH, _silu)]
    plan += [(ws_ref, os_ref, c0, S_WIDTH, None) for c0 in range(0, 3 * S_WIDTH, S_WIDTH)]
    plan += [(ws_ref, os_ref, 3 * S_WIDTH, S_WIDTH, _silu)]
    for w_ref, o_ref, c0, width, gate in plan:
        y = _dot(hb_ref[...], w_ref[:, c0:c0 + width])
        o_ref[:, c0:c0 + width] = (y if gate is None else gate(y)).astype(o_ref.dtype)
    nch = 512
    for c0 in range(0, wg_ref.shape[1], nch):
        y = _dot(hb_ref[...], wg_ref[:, c0:c0 + nch]) + mgb_ref[:, c0:c0 + nch]
        og_ref[:, c0:c0 + nch] = _sigmoid(y).astype(og_ref.dtype)

    ti = lax.broadcasted_iota(jnp.int32, (tm, tm), 0)
    tj = lax.broadcasted_iota(jnp.int32, (tm, tm), 1)
    chunk_shift = CHUNK.bit_length() - 1
    same_chunk = jnp.right_shift(ti, chunk_shift) == jnp.right_shift(tj, chunk_shift)
    upper = jnp.where(same_chunk & (ti <= tj), 1.0, 0.0).astype(BF16)
    lower = jnp.where(same_chunk & (tj <= ti), 1.0, 0.0).astype(BF16)

    gr = _dot_nt(wgt_ref[...], hb_ref[...]) + gbc_ref[...]
    is_i_row = lax.broadcasted_iota(jnp.int32, gr.shape, 0) < M_HEADS
    vr = jnp.where(is_i_row, gr, _log_sigmoid(gr))
    hi, lo = _split_hi_lo(vr)
    csr = _dot(hi, upper) + _dot(lo, upper)
    outr = jnp.where(is_i_row, vr, csr)
    for ci in range(tm // CHUNK):
        ogr_ref[ci] = outr[:, ci * CHUNK:(ci + 1) * CHUNK]

    gc = _dot(hb_ref[...], wgc_ref[...]) + gbr_ref[...]
    is_i_col = lax.broadcasted_iota(jnp.int32, gc.shape, 1) < M_HEADS
    vc = jnp.where(is_i_col, gc, _log_sigmoid(gc))
    hi, lo = _split_hi_lo(vc)
    csc = _dot(lower, hi) + _dot(lower, lo)
    ogc_ref[...] = jnp.where(is_i_col, vc, csc)


def _proj_call(xf, mod_l, ng, wm, wp, ws, wg, wgt, wgc, gbc, gbr, mgb, *, batch):
    n, d = xf.shape
    tm = ROW_TILE
    tiles_per_b = n // batch // tm
    const = lambda i: (0, 0)
    resident = functools.partial(pl.BlockSpec, pipeline_mode=pl.Buffered(1))
    out_shapes = (
        jax.ShapeDtypeStruct((n, wm.shape[1]), BF16),
        jax.ShapeDtypeStruct((n, wp.shape[1]), BF16),
        jax.ShapeDtypeStruct((n, ws.shape[1]), BF16),
        jax.ShapeDtypeStruct((n, wg.shape[1]), BF16),
        jax.ShapeDtypeStruct((n // CHUNK, 2 * M_HEADS, CHUNK), F32),
        jax.ShapeDtypeStruct((n, V7X_LANES), F32),
    )
    return pl.pallas_call(
        _proj_kernel,
        grid=(n // tm,),
        in_specs=[
            pl.BlockSpec((tm, d), lambda i: (i, 0)),
            pl.BlockSpec((1, 1, 3 * d), lambda i: (i // tiles_per_b, 0, 0)),
            pl.BlockSpec((1, d), const),
            resident(wm.shape, const),
            resident(wp.shape, const),
            resident(ws.shape, const),
            resident(wg.shape, const),
            resident(wgt.shape, const),
            resident(wgc.shape, const),
            pl.BlockSpec(gbc.shape, const),
            pl.BlockSpec(gbr.shape, const),
            pl.BlockSpec(mgb.shape, const),
        ],
        out_specs=(
            pl.BlockSpec((tm, wm.shape[1]), lambda i: (i, 0)),
            pl.BlockSpec((tm, wp.shape[1]), lambda i: (i, 0)),
            pl.BlockSpec((tm, ws.shape[1]), lambda i: (i, 0)),
            pl.BlockSpec((tm, wg.shape[1]), lambda i: (i, 0)),
            pl.BlockSpec((tm // CHUNK, 2 * M_HEADS, CHUNK), lambda i: (i, 0, 0)),
            pl.BlockSpec((tm, V7X_LANES), lambda i: (i, 0)),
        ),
        out_shape=out_shapes,
        scratch_shapes=[pltpu.VMEM((tm, d), BF16)],
        compiler_params=pltpu.CompilerParams(
            dimension_semantics=("arbitrary",), vmem_limit_bytes=V7X_VMEM_LIMIT_BYTES),
        name="norm_proj",
    )(xf, mod_l, ng, wm, wp, ws, wg, wgt, wgc, gbc, gbr, mgb)


def _mlstm_kernel(m_ref, gr_ref, gc_ref, cw_ref, cb_ref, ng_ref, o_ref,
                  xbuf, q_s, k_s, vx_s, st_s, m_s, hbuf):
    t = m_ref.shape[0]
    j = pl.program_id(1)
    qk_w = 2 * M_WIDTH
    hd = M_HEAD_DIM

    @pl.when(j == 0)
    def _():
        xbuf[0:V7X_SUBLANES, :] = jnp.zeros((V7X_SUBLANES, qk_w), F32)
        st_s[...] = jnp.zeros_like(st_s)
        m_s[...] = jnp.zeros_like(m_s)
        ones_col = jnp.where(lax.broadcasted_iota(jnp.int32, (t, hd), 1) == 0, 1.0, 0.0).astype(BF16)
        for h in range(M_HEADS):
            vx_s[h, :, hd:2 * hd] = ones_col

    @pl.when(j != 0)
    def _():
        xbuf[0:V7X_SUBLANES, :] = xbuf[t:t + V7X_SUBLANES, :]

    rows = 128
    for r0 in range(0, t, rows):
        xbuf[V7X_SUBLANES + r0:V7X_SUBLANES + r0 + rows, :] = m_ref[r0:r0 + rows, 0:qk_w].astype(F32)
    for r0 in range(0, t, rows):
        acc = cb_ref[...] + cw_ref[0:1, :] * xbuf[r0 + 5:r0 + 5 + rows, :]
        for tap in range(1, CONV_K):
            acc = acc + cw_ref[tap:tap + 1, :] * xbuf[r0 + 5 + tap:r0 + 5 + tap + rows, :]
        act = _silu(acc)
        q_s[r0:r0 + rows, :] = act[:, 0:M_WIDTH].astype(BF16)
        k_s[r0:r0 + rows, :] = (act[:, M_WIDTH:qk_w] * (hd ** -0.5)).astype(BF16)
        for h in range(M_HEADS):
            vx_s[h, r0:r0 + rows, 0:hd] = m_ref[r0:r0 + rows, qk_w + h * hd:qk_w + (h + 1) * hd]

    ti = lax.broadcasted_iota(jnp.int32, (CHUNK, CHUNK), 0)
    si = lax.broadcasted_iota(jnp.int32, (CHUNK, CHUNK), 1)
    tri = si <= ti

    def chunk_body(c, carry):
        r0 = c * CHUNK
        grow = gr_ref[c]
        ig_r = grow[0:M_HEADS, :]
        bc_r = grow[M_HEADS:2 * M_HEADS, :]
        b_end = bc_r[:, CHUNK - 1:CHUNK]
        a_r = b_end - bc_r + ig_r
        a_max = jnp.max(a_r, axis=1, keepdims=True)
        m_prev = m_s[0:M_HEADS, 0:1]
        m_new = jnp.maximum(b_end + m_prev, a_max)
        decay = jnp.exp(b_end + m_prev - m_new)
        inw = jnp.exp(a_max - m_new)
        m_s[0:M_HEADS, 0:1] = m_new
        rr = ig_r - bc_r
        gcol = gc_ref[pl.ds(r0, CHUNK), :]
        for h in range(M_HEADS):
            ig_c = gcol[:, h:h + 1]
            bc_c = gcol[:, M_HEADS + h:M_HEADS + h + 1]
            mp = m_prev[h:h + 1, :]
            q_c = q_s[pl.ds(r0, CHUNK), h * hd:(h + 1) * hd]
            k_c = k_s[pl.ds(r0, CHUNK), h * hd:(h + 1) * hd]
            vx_c = vx_s[h, pl.ds(r0, CHUNK), :]
            log_d = jnp.where(tri, bc_c + rr[h:h + 1, :], -jnp.inf)
            log_inter = bc_c + mp
            m_t = jnp.maximum(log_inter, jnp.max(log_d, axis=1, keepdims=True))
            w_inter = jnp.exp(log_inter - m_t)
            w_intra = jnp.exp(log_d - m_t) * _dot_nt(q_c, k_c)
            inter = _dot(q_c, st_s[h].astype(BF16))
            intra = _dot(w_intra.astype(BF16), vx_c)
            hx = w_inter * inter + intra
            den = hx[:, hd:hd + 1]
            hbuf[pl.ds(r0, CHUNK), h * hd:(h + 1) * hd] = (
                hx[:, 0:hd] / jnp.maximum(jnp.abs(den), jnp.exp(-m_t)))
            wa_c = jnp.exp(b_end[h:h + 1, :] - bc_c + ig_c - a_max[h:h + 1, :])
            vw = (vx_c.astype(F32) * wa_c).astype(BF16)
            st_s[h] = decay[h:h + 1, :] * st_s[h] + inw[h:h + 1, :] * _dot_tn(k_c, vw)
        return carry

    for c in range(t // CHUNK):
        chunk_body(c, 0)

    o_off = qk_w + M_WIDTH
    z_off = o_off + M_WIDTH
    for r0 in range(0, t, rows):
        hg = hbuf[r0:r0 + rows, :] * m_ref[r0:r0 + rows, o_off:o_off + M_WIDTH].astype(F32)
        zg = m_ref[r0:r0 + rows, z_off:z_off + M_WIDTH].astype(F32)
        for h in range(M_HEADS):
            hh = hg[:, h * hd:(h + 1) * hd]
            hn = hh * lax.rsqrt(jnp.mean(hh * hh, axis=-1, keepdims=True) + EPS)
            o_ref[r0:r0 + rows, h * hd:(h + 1) * hd] = (
                hn * ng_ref[:, h * hd:(h + 1) * hd] * zg[:, h * hd:(h + 1) * hd]).astype(o_ref.dtype)


def _mlstm_call(out_m, out_gr, out_gc, conv_w, conv_b, m_norm_g, *, batch):
    n, wm = out_m.shape
    t = ROW_TILE
    tiles_per_b = n // batch // t
    const = lambda b, j: (0, 0)
    return pl.pallas_call(
        _mlstm_kernel,
        grid=(batch, tiles_per_b),
        in_specs=[
            pl.BlockSpec((t, wm), lambda b, j: (b * tiles_per_b + j, 0)),
            pl.BlockSpec((t // CHUNK, 2 * M_HEADS, CHUNK), lambda b, j: (b * tiles_per_b + j, 0, 0)),
            pl.BlockSpec((t, V7X_LANES), lambda b, j: (b * tiles_per_b + j, 0)),
            pl.BlockSpec(conv_w.shape, const),
            pl.BlockSpec(conv_b.shape, const),
            pl.BlockSpec(m_norm_g.shape, const),
        ],
        out_specs=pl.BlockSpec((t, M_WIDTH), lambda b, j: (b * tiles_per_b + j, 0)),
        out_shape=jax.ShapeDtypeStruct((n, M_WIDTH), BF16),
        scratch_shapes=[
            pltpu.VMEM((t + 2 * V7X_SUBLANES, 2 * M_WIDTH), F32),
            pltpu.VMEM((t, M_WIDTH), BF16),
            pltpu.VMEM((t, M_WIDTH), BF16),
            pltpu.VMEM((M_HEADS, t, 2 * M_HEAD_DIM), BF16),
            pltpu.VMEM((M_HEADS, M_HEAD_DIM, 2 * M_HEAD_DIM), F32),
            pltpu.VMEM((V7X_SUBLANES, V7X_LANES), F32),
            pltpu.VMEM((t, M_WIDTH), F32),
        ],
        compiler_params=pltpu.CompilerParams(
            dimension_semantics=("arbitrary", "arbitrary"), vmem_limit_bytes=V7X_VMEM_LIMIT_BYTES),
        name="mlstm",
    )(out_m, out_gr, out_gc, conv_w, conv_b, m_norm_g)


def _sb_kernel(q_ref, k_ref, v_ref, z_ref, o_ref, qm_s, carry_s, acc_s, z_s, zl_s, tot_s, *, dead_log):
    tq = q_ref.shape[0]
    tk = SB_TILE
    i = pl.program_id(1)
    lane_head = jnp.right_shift(lax.broadcasted_iota(jnp.int32, (1, S_WIDTH), 1),
                                S_HEAD_DIM.bit_length() - 1)

    q = q_ref[...] * (S_HEAD_DIM ** -0.5)
    for h in range(S_HEADS):
        qm_s[h * tq:(h + 1) * tq, :] = jnp.where(lane_head == h, q, jnp.zeros_like(q))
    carry_s[...] = jnp.zeros_like(carry_s)
    acc_s[...] = jnp.zeros_like(acc_s)

    uj = lax.broadcasted_iota(jnp.int32, (2 * tk, 2 * tk), 0) & (tk - 1)
    us = lax.broadcasted_iota(jnp.int32, (2 * tk, 2 * tk), 1)
    after = jnp.where((uj > us) | (us >= tk), 1.0, 0.0).astype(BF16)
    rows = S_HEADS * tq
    causal = (lax.broadcasted_iota(jnp.int32, (rows, tk), 1)
              < (lax.broadcasted_iota(jnp.int32, (rows, tk), 0) & (tq - 1)))

    def logits(kt):
        k0 = pl.multiple_of(kt * tk, tk)
        z_s[...] = _dot_nt(qm_s[...], k_ref[pl.ds(k0, tk), :])

    def survival(diag):
        z = z_s[...]
        log_fail =-(jnp.maximum(z, 0.0) + jnp.log(1.0 + jnp.exp(-jnp.abs(z))))
        if diag:
            log_fail = jnp.where(causal, log_fail, 0.0)
        hi, lo = _split_hi_lo(log_fail)
        cum = _dot(jnp.concatenate([hi, lo], axis=1), after)
        zl = z + log_fail + cum[:, 0:tk]
        if diag:
            zl = jnp.where(causal, zl, -jnp.inf)
        zl_s[...] = zl
        tot_s[...] = cum[:, tk:2 * tk]

    def accumulate(kt):
        k0 = pl.multiple_of(kt * tk, tk)
        carry = carry_s[...]
        attn = jnp.exp(zl_s[...] + carry).astype(BF16)
        vv = v_ref[pl.ds(k0, tk), :]
        attn_wide = jnp.concatenate([attn[h * tq:(h + 1) * tq, :] for h in range(S_HEADS)], axis=1)
        v_heads = jnp.concatenate(
            [jnp.where(lane_head == h, vv, jnp.zeros_like(vv)) for h in range(S_HEADS)], axis=0)
        acc_s[...] += _dot(attn_wide, v_heads)
        new_carry = carry + tot_s[...]
        carry_s[...] = new_carry
        return jnp.max(new_carry)

    logits(i)
    survival(True)
    logits(jnp.maximum(i - 1, 0))

    def cond(state):
        kt, alive = state
        return (kt >= 0) & (alive > dead_log)

    def body(state):
        kt, _ = state
        alive = accumulate(kt)
        survival(False)
        logits(jnp.maximum(kt - 2, 0))
        return kt - 1, alive

    lax.while_loop(cond, body, (i, jnp.float32(0.0)))

    o_ref[...] = (acc_s[...] * z_ref[...].astype(F32)).astype(o_ref.dtype)


def _sb_call(out_s, *, batch, dead_log):
    n, _ = out_s.shape
    s = n // batch
    tq = SB_TILE
    nq = s // tq
    return pl.pallas_call(
        functools.partial(_sb_kernel, dead_log=dead_log),
        grid=(batch, nq),
        in_specs=[
            pl.BlockSpec((tq, S_WIDTH), lambda b, i: (b * nq + i, 0)),
            pl.BlockSpec((s, S_WIDTH), lambda b, i: (b, 1)),
            pl.BlockSpec((s, S_WIDTH), lambda b, i: (b, 2)),
            pl.BlockSpec((tq, S_WIDTH), lambda b, i: (b * nq + i, 3)),
        ],
        out_specs=pl.BlockSpec((tq, S_WIDTH), lambda b, i: (b * nq + i, 0)),
        out_shape=jax.ShapeDtypeStruct((n, S_WIDTH), BF16),
        scratch_shapes=[
            pltpu.VMEM((S_HEADS * tq, S_WIDTH), BF16),
            pltpu.VMEM((S_HEADS * tq, SB_TILE), F32),
            pltpu.VMEM((tq, S_WIDTH), F32),
            pltpu.VMEM((S_HEADS * tq, SB_TILE), F32),
            pltpu.VMEM((S_HEADS * tq, SB_TILE), F32),
            pltpu.VMEM((S_HEADS * tq, SB_TILE), F32),
        ],
        compiler_params=pltpu.CompilerParams(
            dimension_semantics=("arbitrary", "arbitrary"), vmem_limit_bytes=V7X_VMEM_LIMIT_BYTES),
        name="stick_breaking",
    )(out_s, out_s, out_s, out_s)


def _merge_kernel(x_ref, mod_ref, ym_ref, ys_ref, p_ref, pprev_ref, g_ref,
                  wm_ref, wp_ref, ws_ref, wo_ref, pw_ref, ps_ref, fg_ref,
                  o_ref, pa_s, pb_s, yp_s, *, tiles_per_b, final):
    tm, d = x_ref.shape
    i = pl.program_id(0)
    jb = i % tiles_per_b
    top = POOL_PAD + POOL_HALO

    pa_s[0:POOL_PAD, :] = jnp.zeros((POOL_PAD, P_WIDTH), F32)
    pb_s[0:POOL_PAD, :] = jnp.zeros((POOL_PAD, P_WIDTH), F32)
    halo = pprev_ref[:, 0:P_WIDTH].astype(F32)
    pa_s[POOL_PAD:top, :] = jnp.where(jb == 0, jnp.zeros_like(halo), halo)
    u = p_ref[:, 0:P_WIDTH].astype(F32)
    pa_s[top:top + tm, :] = u
    nrow = POOL_HALO + tm
    lane_group = jnp.right_shift(lax.broadcasted_iota(jnp.int32, (1, P_WIDTH), 1),
                                 P_GROUP_DIM.bit_length() - 1)
    src, dst = pa_s, pb_s
    wsum = None
    for level, shift in enumerate((1, 2, 4, 8)):
        summed = src[POOL_PAD:POOL_PAD + nrow, :] + src[POOL_PAD - shift:POOL_PAD - shift + nrow, :]
        dst[POOL_PAD:POOL_PAD + nrow, :] = summed
        cur = dst[top:top + tm, :]
        wsum = cur if wsum is None else jnp.where(lane_group >= level, cur, wsum)
        src, dst = dst, src
    window = jnp.left_shift(2, lane_group).astype(F32)
    tpos = (jb * tm + lax.broadcasted_iota(jnp.int32, (tm, 1), 0) + 1).astype(F32)
    count = jnp.minimum(tpos, window)
    pooled = wsum / count - u
    pm = _dot(pooled.astype(BF16), pw_ref[...])
    yp_s[...] = (pm * ps_ref[...] * p_ref[:, P_WIDTH:2 * P_WIDTH].astype(F32)).astype(BF16)

    gate = mod_ref[0, :, 2 * d:3 * d]
    rows = 256
    for r0 in range(0, tm, rows):
        rs = slice(r0, r0 + rows)
        merged = (g_ref[rs, 0:d].astype(F32) * _dot(ym_ref[rs, :], wm_ref[...])
                  + g_ref[rs, d:2 * d].astype(F32) * _dot(yp_s[rs, :], wp_ref[...])
                  + g_ref[rs, 2 * d:3 * d].astype(F32) * _dot(ys_ref[rs, :], ws_ref[...]))
        xn = x_ref[rs, :] + gate * _dot(merged.astype(BF16), wo_ref[...])
        if final:
            xn = xn * lax.rsqrt(jnp.mean(xn * xn, axis=-1, keepdims=True) + EPS) * fg_ref[...]
        o_ref[rs, :] = xn


def _merge_call(xf, mod_l, y_m, y_s, out_p, out_g, wbm, wbp, wbs, wo, pw_bd, pscale, final_g,
                *, batch, final):
    n, d = xf.shape
    tm = ROW_TILE
    tiles_per_b = n // batch // tm
    halo_blocks = tm // POOL_HALO
    const = lambda i: (0, 0)
    resident = functools.partial(pl.BlockSpec, pipeline_mode=pl.Buffered(1))
    return pl.pallas_call(
        functools.partial(_merge_kernel, tiles_per_b=tiles_per_b, final=final),
        grid=(n // tm,),
        in_specs=[
            pl.BlockSpec((tm, d), lambda i: (i, 0)),
            pl.BlockSpec((1, 1, 3 * d), lambda i: (i // tiles_per_b, 0, 0)),
            pl.BlockSpec((tm, M_WIDTH), lambda i: (i, 0)),
            pl.BlockSpec((tm, S_WIDTH), lambda i: (i, 0)),
            pl.BlockSpec((tm, 2 * P_WIDTH), lambda i: (i, 0)),
            pl.BlockSpec((POOL_HALO, 2 * P_WIDTH), lambda i: (jnp.maximum(i * halo_blocks - 1, 0), 0)),
            pl.BlockSpec((tm, N_BRANCH * d), lambda i: (i, 0)),
            resident(wbm.shape, const),
            resident(wbp.shape, const),
            resident(wbs.shape, const),
            resident(wo.shape, const),
            resident(pw_bd.shape, const),
            pl.BlockSpec((1, P_WIDTH), const),
            pl.BlockSpec((1, d), const),
        ],
        out_specs=pl.BlockSpec((tm, d), lambda i: (i, 0)),
        out_shape=jax.ShapeDtypeStruct((n, d), F32),
        scratch_shapes=[
            pltpu.VMEM((POOL_PAD + POOL_HALO + tm, P_WIDTH), F32),
            pltpu.VMEM((POOL_PAD + POOL_HALO + tm, P_WIDTH), F32),
            pltpu.VMEM((tm, P_WIDTH), BF16),
        ],
        compiler_params=pltpu.CompilerParams(
            dimension_semantics=("arbitrary",), vmem_limit_bytes=V7X_VMEM_LIMIT_BYTES),
        name="pool_merge",
    )(xf, mod_l, y_m, y_s, out_p, out_p, out_g, wbm, wbp, wbs, wo, pw_bd, pscale, final_g)


def _split_w_in(w):
    o = 0
    sizes = dict(mq=M_WIDTH, mk=M_WIDTH, mv=M_WIDTH, mi=M_HEADS, mf=M_HEADS, mo=M_WIDTH, mz=M_WIDTH,
                 pu=P_WIDTH, pz=P_WIDTH, sq=S_WIDTH, sk=S_WIDTH, sv=S_WIDTH, sz=S_WIDTH)
    cols = {}
    for name, sz in sizes.items():
        cols[name] = w[:, o:o + sz]
        o += sz
    cols["gpre"] = w[:, o:]
    return cols


def kernel(x, c, norm_g, w_ada, b_ada, w_in, m_gate_b, conv_w, conv_b, m_norm_g, pool_w, pool_scale,
           w_br_m, w_br_p, w_br_s, gate_b, w_out, final_g):
    batch, seq, d = x.shape
    depth = w_in.shape[0]
    n = batch * seq
    assert seq % ROW_TILE == 0 and seq % SB_TILE == 0 and d % V7X_LANES == 0

    c_pad = jnp.zeros((V7X_SUBLANES, d), F32).at[:batch].set(c)
    mod = _ada_call(c_pad, w_ada, b_ada)

    xf = x.reshape(n, d)
    for l in range(depth):
        mod_l = mod[l, :batch].reshape(batch, 1, 3 * d)
        cols = _split_w_in(w_in[l])
        wm = jnp.concatenate([cols[k] for k in ("mq", "mk", "mv", "mo", "mz")], axis=1).astype(BF16)
        wp = jnp.concatenate([cols["pu"], cols["pz"]], axis=1).astype(BF16)
        ws = jnp.concatenate([cols[k] for k in ("sq", "sk", "sv", "sz")], axis=1).astype(BF16)
        wg = cols["gpre"].astype(BF16)
        w_gate = jnp.concatenate([cols["mi"], cols["mf"]], axis=1)
        wgt = w_gate.T.astype(BF16)
        wgc = jnp.zeros((d, V7X_LANES), F32).at[:, :2 * M_HEADS].set(w_gate).astype(BF16)
        gbc = m_gate_b[l].reshape(2 * M_HEADS, 1)
        gbr = jnp.zeros((1, V7X_LANES), F32).at[0, :2 * M_HEADS].set(m_gate_b[l])

        out_m, out_p, out_s, out_g, out_gr, out_gc = _proj_call(
            xf, mod_l, norm_g[l].reshape(1, d), wm, wp, ws, wg, wgt, wgc, gbc, gbr,
            gate_b[l].reshape(1, -1), batch=batch)

        y_m = _mlstm_call(out_m, out_gr, out_gc, conv_w[l], conv_b[l].reshape(1, -1),
                          m_norm_g[l].reshape(1, -1), batch=batch)
        y_s = _sb_call(out_s, batch=batch, dead_log=SB_DEAD_LOG)

        pw_bd = jnp.zeros((P_WIDTH, P_WIDTH), F32)
        for g in range(P_GROUPS):
            sl = slice(g * P_GROUP_DIM, (g + 1) * P_GROUP_DIM)
            pw_bd = pw_bd.at[sl, sl].set(pool_w[l, g])
        xf = _merge_call(
            xf, mod_l, y_m, y_s, out_p, out_g,
            w_br_m[l].astype(BF16), w_br_p[l].astype(BF16), w_br_s[l].astype(BF16),
            w_out[l].astype(BF16), pw_bd.astype(BF16), pool_scale[l].reshape(1, -1),
            final_g.reshape(1, d), batch=batch, final=(l == depth - 1))
    return xf.reshape(batch, seq, d)
```

```python
import functools

import jax
import jax.numpy as jnp
from jax import lax
from jax.experimental import pallas as pl
from jax.experimental.pallas import tpu as pltpu

F32 = jnp.float32
BF16 = jnp.bfloat16

EPS = 1e-6
CHUNK = 256
M_HEADS = 4
M_HEAD_DIM = 128
M_WIDTH = M_HEADS * M_HEAD_DIM
CONV_K = 4
P_GROUPS = 4
P_GROUP_DIM = 64
P_WIDTH = P_GROUPS * P_GROUP_DIM
POOL_WINDOWS = (2, 4, 8, 16)
S_HEADS = 4
S_HEAD_DIM = 64
S_WIDTH = S_HEADS * S_HEAD_DIM
N_BRANCH = 3

V7X_LANES = 128
V7X_SUBLANES = 8
V7X_VMEM_LIMIT_BYTES = 56 * 1024 * 1024

ROW_TILE = 512
SB_TILE = 128
SB_SUBTILES = 4
POOL_HALO = 16
POOL_PAD = 8

SB_DEAD_LOG = -120.0


def _sigmoid(x):
    return 1.0 / (1.0 + jnp.exp(-x))


def _silu(x):
    return x * _sigmoid(x)


def _log_sigmoid(x):
    return jnp.minimum(x, 0.0) - jnp.log1p(jnp.exp(-jnp.abs(x)))


def _split_hi_lo(a):
    hi = a.astype(BF16)
    lo = (a - hi.astype(F32)).astype(BF16)
    return hi, lo


def _dot(a, b):
    return jnp.dot(a, b, preferred_element_type=F32)


def _dot_nt(a, b):
    return lax.dot_general(a, b, (((1,), (1,)), ((), ())), preferred_element_type=F32)


def _dot_tn(a, b):
    return lax.dot_general(a, b, (((0,), (0,)), ((), ())), preferred_element_type=F32)


def _ada_kernel(c_ref, w_ref, b_ref, o_ref):
    o_ref[0] = jnp.dot(c_ref[...], w_ref[0], preferred_element_type=F32,
                       precision=lax.Precision.HIGHEST) + b_ref[0]


def _ada_call(c_pad, w_ada, b_ada):
    depth, d, n3 = w_ada.shape
    tn = 1024
    return pl.pallas_call(
        _ada_kernel,
        grid=(depth, n3 // tn),
        in_specs=[
            pl.BlockSpec((c_pad.shape[0], d), lambda l, j: (0, 0)),
            pl.BlockSpec((1, d, tn), lambda l, j: (l, 0, j)),
            pl.BlockSpec((1, 1, tn), lambda l, j: (l, 0, j)),
        ],
        out_specs=pl.BlockSpec((1, c_pad.shape[0], tn), lambda l, j: (l, 0, j)),
        out_shape=jax.ShapeDtypeStruct((depth, c_pad.shape[0], n3), F32),
        compiler_params=pltpu.CompilerParams(
            dimension_semantics=("arbitrary", "arbitrary")),
        name="ada_mod",
    )(c_pad, w_ada, b_ada.reshape(depth, 1, n3))


W_OFF_M = 0
W_OFF_P = 5 * M_WIDTH
W_OFF_S = W_OFF_P + 2 * P_WIDTH
W_OFF_G = W_OFF_S + 4 * S_WIDTH


def _proj_kernel(x_ref, mod_ref, ng_ref, w_ref, wgt_ref, wgc_ref, gbc_ref, gbr_ref, mgb_ref,
                 upper_ref, lower_ref,
                 om_ref, op_ref, os_ref, og_ref, ogr_ref, ogc_ref,
                 hb_ref):
    tm, d = x_ref.shape
    g = ng_ref[...]
    scale1 = 1.0 + mod_ref[0, :, d:2 * d]
    shift = mod_ref[0, :, 0:d]
    rc = 64
    for r0 in range(0, tm, rc):
        xf = x_ref[r0:r0 + rc, :]
        y = xf * lax.rsqrt(jnp.mean(xf * xf, axis=-1, keepdims=True) + EPS)
        y = y * g
        hb_ref[r0:r0 + rc, :] = (y * scale1 + shift).astype(BF16)

    gr = _dot_nt(wgt_ref[...], hb_ref[...]) + gbc_ref[...]
    is_i_row = lax.broadcasted_iota(jnp.int32, gr.shape, 0) < M_HEADS
    vr = jnp.where(is_i_row, gr, _log_sigmoid(gr))
    hi, lo = _split_hi_lo(vr)
    csr = _dot(hi, upper_ref[...]) + _dot(lo, upper_ref[...])
    outr = jnp.where(is_i_row, vr, csr)
    for ci in range(tm // CHUNK):
        ogr_ref[ci] = outr[:, ci * CHUNK:(ci + 1) * CHUNK]

    gc = _dot(hb_ref[...], wgc_ref[...]) + gbr_ref[...]
    is_i_col = lax.broadcasted_iota(jnp.int32, gc.shape, 1) < M_HEADS
    vc = jnp.where(is_i_col, gc, _log_sigmoid(gc))
    hi, lo = _split_hi_lo(vc)
    csc = _dot(lower_ref[...], hi) + _dot(lower_ref[...], lo)
    ogc_ref[...] = jnp.where(is_i_col, vc, csc)

    plan = [(om_ref, W_OFF_M, c0, M_WIDTH, None) for c0 in range(0, 3 * M_WIDTH, M_WIDTH)]
    plan += [(om_ref, W_OFF_M, 3 * M_WIDTH, M_WIDTH, _sigmoid),
             (om_ref, W_OFF_M, 4 * M_WIDTH, M_WIDTH, _silu),
             (op_ref, W_OFF_P, 0, P_WIDTH, None), (op_ref, W_OFF_P, P_WIDTH, P_WIDTH, _silu)]
    plan += [(os_ref, W_OFF_S, c0, S_WIDTH, None) for c0 in range(0, 3 * S_WIDTH, S_WIDTH)]
    plan += [(os_ref, W_OFF_S, 3 * S_WIDTH, S_WIDTH, _silu)]
    for o_ref, w0, c0, width, gate in plan:
        y = _dot(hb_ref[...], w_ref[:, w0 + c0:w0 + c0 + width])
        o_ref[:, c0:c0 + width] = (y if gate is None else gate(y)).astype(o_ref.dtype)
    nch = 512
    for c0 in range(0, og_ref.shape[1], nch):
        y = _dot(hb_ref[...], w_ref[:, W_OFF_G + c0:W_OFF_G + c0 + nch]) + mgb_ref[:, c0:c0 + nch]
        og_ref[:, c0:c0 + nch] = _sigmoid(y).astype(og_ref.dtype)


def _chunk_prefix_matrices(tm):
    idx = jnp.arange(tm)
    same = (idx[:, None] // CHUNK) == (idx[None, :] // CHUNK)
    upper = same & (idx[:, None] <= idx[None, :])
    return upper.astype(BF16), upper.T.astype(BF16)


def _proj_call(xf, mod_l, ng, w, wgt, wgc, gbc, gbr, mgb, *, batch):
    n, d = xf.shape
    tm = ROW_TILE
    tiles_per_b = n // batch // tm
    upper, lower = _chunk_prefix_matrices(tm)
    const = lambda i: (0, 0)
    resident = functools.partial(pl.BlockSpec, pipeline_mode=pl.Buffered(1))
    out_shapes = (
        jax.ShapeDtypeStruct((n, W_OFF_P - W_OFF_M), BF16),
        jax.ShapeDtypeStruct((n, W_OFF_S - W_OFF_P), BF16),
        jax.ShapeDtypeStruct((n, W_OFF_G - W_OFF_S), BF16),
        jax.ShapeDtypeStruct((n, w.shape[1] - W_OFF_G), BF16),
        jax.ShapeDtypeStruct((n // CHUNK, 2 * M_HEADS, CHUNK), F32),
        jax.ShapeDtypeStruct((n, V7X_LANES), F32),
    )
    return pl.pallas_call(
        _proj_kernel,
        grid=(n // tm,),
        in_specs=[
            pl.BlockSpec((tm, d), lambda i: (i, 0)),
            pl.BlockSpec((1, 1, 3 * d), lambda i: (i // tiles_per_b, 0, 0)),
            pl.BlockSpec((1, d), const),
            resident(w.shape, const),
            resident(wgt.shape, const),
            resident(wgc.shape, const),
            pl.BlockSpec(gbc.shape, const),
            pl.BlockSpec(gbr.shape, const),
            pl.BlockSpec(mgb.shape, const),
            resident(upper.shape, const),
            resident(lower.shape, const),
        ],
        out_specs=tuple(pl.BlockSpec((tm, s.shape[1]), lambda i: (i, 0)) for s in out_shapes[:4]) + (
            pl.BlockSpec((tm // CHUNK, 2 * M_HEADS, CHUNK), lambda i: (i, 0, 0)),
            pl.BlockSpec((tm, V7X_LANES), lambda i: (i, 0)),
        ),
        out_shape=out_shapes,
        scratch_shapes=[pltpu.VMEM((tm, d), BF16)],
        compiler_params=pltpu.CompilerParams(
            dimension_semantics=("arbitrary",), vmem_limit_bytes=V7X_VMEM_LIMIT_BYTES),
        name="norm_proj",
    )(xf, mod_l, ng, w, wgt, wgc, gbc, gbr, mgb, upper, lower)


def _mlstm_kernel(m_ref, gr_ref, gc_ref, cw_ref, cb_ref, ng_ref, o_ref,
                  xbuf, q_s, k_s, vx_s, st_s, m_s, hbuf):
    t = m_ref.shape[0]
    j = pl.program_id(1)
    qk_w = 2 * M_WIDTH
    hd = M_HEAD_DIM

    @pl.when(j == 0)
    def _():
        xbuf[0:V7X_SUBLANES, :] = jnp.zeros((V7X_SUBLANES, qk_w), F32)
        st_s[...] = jnp.zeros_like(st_s)
        m_s[...] = jnp.zeros_like(m_s)
        ones_col = jnp.where(lax.broadcasted_iota(jnp.int32, (t, hd), 1) == 0, 1.0, 0.0).astype(BF16)
        for h in range(M_HEADS):
            vx_s[h, :, hd:2 * hd] = ones_col

    @pl.when(j != 0)
    def _():
        xbuf[0:V7X_SUBLANES, :] = xbuf[t:t + V7X_SUBLANES, :]

    rows = 128
    for r0 in range(0, t, rows):
        xbuf[V7X_SUBLANES + r0:V7X_SUBLANES + r0 + rows, :] = m_ref[r0:r0 + rows, 0:qk_w].astype(F32)
    for r0 in range(0, t, rows):
        acc = cb_ref[...] + cw_ref[0:1, :] * xbuf[r0 + 5:r0 + 5 + rows, :]
        for tap in range(1, CONV_K):
            acc = acc + cw_ref[tap:tap + 1, :] * xbuf[r0 + 5 + tap:r0 + 5 + tap + rows, :]
        act = _silu(acc)
        q_s[r0:r0 + rows, :] = act[:, 0:M_WIDTH].astype(BF16)
        k_s[r0:r0 + rows, :] = (act[:, M_WIDTH:qk_w] * (hd ** -0.5)).astype(BF16)
        for h in range(M_HEADS):
            vx_s[h, r0:r0 + rows, 0:hd] = m_ref[r0:r0 + rows, qk_w + h * hd:qk_w + (h + 1) * hd]

    ti = lax.broadcasted_iota(jnp.int32, (CHUNK, CHUNK), 0)
    si = lax.broadcasted_iota(jnp.int32, (CHUNK, CHUNK), 1)
    tri = si <= ti

    def chunk_body(c, carry):
        r0 = c * CHUNK
        grow = gr_ref[c]
        ig_r = grow[0:M_HEADS, :]
        bc_r = grow[M_HEADS:2 * M_HEADS, :]
        b_end = bc_r[:, CHUNK - 1:CHUNK]
        a_r = b_end - bc_r + ig_r
        a_max = jnp.max(a_r, axis=1, keepdims=True)
        m_prev = m_s[0:M_HEADS, 0:1]
        m_new = jnp.maximum(b_end + m_prev, a_max)
        decay = jnp.exp(b_end + m_prev - m_new)
        inw = jnp.exp(a_max - m_new)
        m_s[0:M_HEADS, 0:1] = m_new
        rr = ig_r - bc_r
        gcol = gc_ref[pl.ds(r0, CHUNK), :]
        for h in range(M_HEADS):
            ig_c = gcol[:, h:h + 1]
            bc_c = gcol[:, M_HEADS + h:M_HEADS + h + 1]
            mp = m_prev[h:h + 1, :]
            q_c = q_s[pl.ds(r0, CHUNK), h * hd:(h + 1) * hd]
            k_c = k_s[pl.ds(r0, CHUNK), h * hd:(h + 1) * hd]
            vx_c = vx_s[h, pl.ds(r0, CHUNK), :]
            log_d = jnp.where(tri, bc_c + rr[h:h + 1, :], -jnp.inf)
            log_inter = bc_c + mp
            m_t = jnp.maximum(log_inter, jnp.max(log_d, axis=1, keepdims=True))
            w_inter = jnp.exp(log_inter - m_t)
            w_intra = jnp.exp(log_d - m_t) * _dot_nt(q_c, k_c)
            inter = _dot(q_c, st_s[h].astype(BF16))
            intra = _dot(w_intra.astype(BF16), vx_c)
            hx = w_inter * inter + intra
            den = hx[:, hd:hd + 1]
            hbuf[pl.ds(r0, CHUNK), h * hd:(h + 1) * hd] = (
                hx[:, 0:hd] / jnp.maximum(jnp.abs(den), jnp.exp(-m_t)))
            wa_c = jnp.exp(b_end[h:h + 1, :] - bc_c + ig_c - a_max[h:h + 1, :])
            vw = (vx_c.astype(F32) * wa_c).astype(BF16)
            st_s[h] = decay[h:h + 1, :] * st_s[h] + inw[h:h + 1, :] * _dot_tn(k_c, vw)
        return carry

    for c in range(t // CHUNK):
        chunk_body(c, 0)

    o_off = qk_w + M_WIDTH
    z_off = o_off + M_WIDTH
    for r0 in range(0, t, rows):
        hg = hbuf[r0:r0 + rows, :] * m_ref[r0:r0 + rows, o_off:o_off + M_WIDTH].astype(F32)
        zg = m_ref[r0:r0 + rows, z_off:z_off + M_WIDTH].astype(F32)
        for h in range(M_HEADS):
            hh = hg[:, h * hd:(h + 1) * hd]
            hn = hh * lax.rsqrt(jnp.mean(hh * hh, axis=-1, keepdims=True) + EPS)
            o_ref[r0:r0 + rows, h * hd:(h + 1) * hd] = (
                hn * ng_ref[:, h * hd:(h + 1) * hd] * zg[:, h * hd:(h + 1) * hd]).astype(o_ref.dtype)


def _mlstm_call(out_m, out_gr, out_gc, conv_w, conv_b, m_norm_g, *, batch):
    n, wm = out_m.shape
    t = ROW_TILE
    tiles_per_b = n // batch // t
    const = lambda b, j: (0, 0)
    return pl.pallas_call(
        _mlstm_kernel,
        grid=(batch, tiles_per_b),
        in_specs=[
            pl.BlockSpec((t, wm), lambda b, j: (b * tiles_per_b + j, 0)),
            pl.BlockSpec((t // CHUNK, 2 * M_HEADS, CHUNK), lambda b, j: (b * tiles_per_b + j, 0, 0)),
            pl.BlockSpec((t, V7X_LANES), lambda b, j: (b * tiles_per_b + j, 0)),
            pl.BlockSpec(conv_w.shape, const),
            pl.BlockSpec(conv_b.shape, const),
            pl.BlockSpec(m_norm_g.shape, const),
        ],
        out_specs=pl.BlockSpec((t, M_WIDTH), lambda b, j: (b * tiles_per_b + j, 0)),
        out_shape=jax.ShapeDtypeStruct((n, M_WIDTH), BF16),
        scratch_shapes=[
            pltpu.VMEM((t + 2 * V7X_SUBLANES, 2 * M_WIDTH), F32),
            pltpu.VMEM((t, M_WIDTH), BF16),
            pltpu.VMEM((t, M_WIDTH), BF16),
            pltpu.VMEM((M_HEADS, t, 2 * M_HEAD_DIM), BF16),
            pltpu.VMEM((M_HEADS, M_HEAD_DIM, 2 * M_HEAD_DIM), F32),
            pltpu.VMEM((V7X_SUBLANES, V7X_LANES), F32),
            pltpu.VMEM((t, M_WIDTH), F32),
        ],
        compiler_params=pltpu.CompilerParams(
            dimension_semantics=("arbitrary", "arbitrary"), vmem_limit_bytes=V7X_VMEM_LIMIT_BYTES),
        name="mlstm",
    )(out_m, out_gr, out_gc, conv_w, conv_b, m_norm_g)


def _sb_kernel(q_ref, k_ref, v_ref, z_ref, after_ref, o_ref, qm_s, carry_s, acc_s, z_s, zl_s, tot_s, *, dead_log):
    tq = tk = SB_TILE
    nsub = q_ref.shape[0] // tq
    hrows = S_HEADS * tq
    i = pl.program_id(1)
    lane_head = jnp.right_shift(lax.broadcasted_iota(jnp.int32, (1, S_WIDTH), 1),
                                S_HEAD_DIM.bit_length() - 1)

    for u in range(nsub):
        q = q_ref[u * tq:(u + 1) * tq, :] * (S_HEAD_DIM ** -0.5)
        for h in range(S_HEADS):
            qm_s[u * hrows + h * tq:u * hrows + (h + 1) * tq, :] = (
                jnp.where(lane_head == h, q, jnp.zeros_like(q)))
    carry_s[...] = jnp.zeros_like(carry_s)
    acc_s[...] = jnp.zeros_like(acc_s)

    after = after_ref[...]
    rows = nsub * hrows
    causal = (lax.broadcasted_iota(jnp.int32, (rows, tk), 1)
              < (lax.broadcasted_iota(jnp.int32, (rows, tk), 0) & (tq - 1)))

    def key_rows(u, r):
        kt = jnp.maximum(nsub * i + u - r, 0)
        return pl.ds(pl.multiple_of(kt * tk, tk), tk)

    def logits(r):
        for u in range(nsub):
            z_s[u * hrows:(u + 1) * hrows, :] = _dot_nt(
                qm_s[u * hrows:(u + 1) * hrows, :], k_ref[key_rows(u, r), :])

    def survival(diag):
        z = z_s[...]
        log_fail =-(jnp.maximum(z, 0.0) + jnp.log(1.0 + jnp.exp(-jnp.abs(z))))
        if diag:
            log_fail = jnp.where(causal, log_fail, 0.0)
        hi, lo = _split_hi_lo(log_fail)
        cum = _dot(jnp.concatenate([hi, lo], axis=1), after)
        zl = z + log_fail + cum[:, 0:tk]
        if diag:
            zl = jnp.where(causal, zl, -jnp.inf)
        zl_s[...] = zl
        tot_s[...] = cum[:, tk:2 * tk]

    def accumulate(r):
        alive = None
        for u in range(nsub):
            rs = slice(u * hrows, (u + 1) * hrows)
            carry = carry_s[rs, :]
            attn = jnp.exp(zl_s[rs, :] + carry).astype(BF16)
            vv = v_ref[key_rows(u, r), :]
            attn_wide = jnp.concatenate([attn[h * tq:(h + 1) * tq, :] for h in range(S_HEADS)], axis=1)
            v_heads = jnp.concatenate(
                [jnp.where(lane_head == h, vv, jnp.zeros_like(vv)) for h in range(S_HEADS)], axis=0)
            acc_s[u * tq:(u + 1) * tq, :] += _dot(attn_wide, v_heads)
            has_more = nsub * i + u - r > 0
            new_carry = jnp.where(has_more, carry + tot_s[rs, :], -jnp.inf)
            carry_s[rs, :] = new_carry
            alive = new_carry if alive is None else jnp.maximum(alive, new_carry)
        return jnp.max(alive)

    logits(0)
    survival(True)
    logits(1)

    last = nsub * i + nsub - 1

    def cond(state):
        r, alive = state
        return (r <= last) & (alive > dead_log)

    def body(state):
        r, _ = state
        alive = accumulate(r)
        survival(False)
        logits(r + 2)
        return r + 1, alive

    lax.while_loop(cond, body, (jnp.int32(0), jnp.float32(0.0)))

    o_ref[...] = (acc_s[...] * z_ref[...].astype(F32)).astype(o_ref.dtype)


def _sb_call(out_s, *, batch, dead_log):
    n, _ = out_s.shape
    s = n // batch
    tq = SB_SUBTILES * SB_TILE
    nq = s // tq
    kj = jnp.arange(2 * SB_TILE)[:, None] % SB_TILE
    ks = jnp.arange(2 * SB_TILE)[None, :]
    after = ((kj > ks) | (ks >= SB_TILE)).astype(BF16)
    return pl.pallas_call(
        functools.partial(_sb_kernel, dead_log=dead_log),
        grid=(batch, nq),
        in_specs=[
            pl.BlockSpec((tq, S_WIDTH), lambda b, i: (b * nq + i, 0)),
            pl.BlockSpec((s, S_WIDTH), lambda b, i: (b, 1)),
            pl.BlockSpec((s, S_WIDTH), lambda b, i: (b, 2)),
            pl.BlockSpec((tq, S_WIDTH), lambda b, i: (b * nq + i, 3)),
            pl.BlockSpec(after.shape, lambda b, i: (0, 0)),
        ],
        out_specs=pl.BlockSpec((tq, S_WIDTH), lambda b, i: (b * nq + i, 0)),
        out_shape=jax.ShapeDtypeStruct((n, S_WIDTH), BF16),
        scratch_shapes=[
            pltpu.VMEM((S_HEADS * tq, S_WIDTH), BF16),
            pltpu.VMEM((S_HEADS * tq, SB_TILE), F32),
            pltpu.VMEM((tq, S_WIDTH), F32),
            pltpu.VMEM((S_HEADS * tq, SB_TILE), F32),
            pltpu.VMEM((S_HEADS * tq, SB_TILE), F32),
            pltpu.VMEM((S_HEADS * tq, SB_TILE), F32),
        ],
        compiler_params=pltpu.CompilerParams(
            dimension_semantics=("arbitrary", "arbitrary"), vmem_limit_bytes=V7X_VMEM_LIMIT_BYTES),
        name="stick_breaking",
    )(out_s, out_s, out_s, out_s, after)


def _merge_kernel(x_ref, mod_ref, ym_ref, ys_ref, p_ref, pprev_ref, g_ref,
                  wm_ref, wp_ref, ws_ref, wo_ref, pw_ref, ps_ref, fg_ref,
                  o_ref, pa_s, pb_s, yp_s, *, tiles_per_b, final):
    tm, d = x_ref.shape
    i = pl.program_id(0)
    jb = i % tiles_per_b
    top = POOL_PAD + POOL_HALO

    pa_s[0:POOL_PAD, :] = jnp.zeros((POOL_PAD, P_WIDTH), F32)
    pb_s[0:POOL_PAD, :] = jnp.zeros((POOL_PAD, P_WIDTH), F32)
    halo = pprev_ref[:, 0:P_WIDTH].astype(F32)
    pa_s[POOL_PAD:top, :] = jnp.where(jb == 0, jnp.zeros_like(halo), halo)
    u = p_ref[:, 0:P_WIDTH].astype(F32)
    pa_s[top:top + tm, :] = u
    nrow = POOL_HALO + tm
    lane_group = jnp.right_shift(lax.broadcasted_iota(jnp.int32, (1, P_WIDTH), 1),
                                 P_GROUP_DIM.bit_length() - 1)
    src, dst = pa_s, pb_s
    wsum = None
    for level, shift in enumerate((1, 2, 4, 8)):
        summed = src[POOL_PAD:POOL_PAD + nrow, :] + src[POOL_PAD - shift:POOL_PAD - shift + nrow, :]
        dst[POOL_PAD:POOL_PAD + nrow, :] = summed
        cur = dst[top:top + tm, :]
        wsum = cur if wsum is None else jnp.where(lane_group >= level, cur, wsum)
        src, dst = dst, src
    window = jnp.left_shift(2, lane_group).astype(F32)
    tpos = (jb * tm + lax.broadcasted_iota(jnp.int32, (tm, 1), 0) + 1).astype(F32)
    count = jnp.minimum(tpos, window)
    pooled = wsum / count - u
    pm = _dot(pooled.astype(BF16), pw_ref[...])
    yp_s[...] = (pm * ps_ref[...] * p_ref[:, P_WIDTH:2 * P_WIDTH].astype(F32)).astype(BF16)

    gate = mod_ref[0, :, 2 * d:3 * d]
    rows = 256
    for r0 in range(0, tm, rows):
        rs = slice(r0, r0 + rows)
        merged = (g_ref[rs, 0:d].astype(F32) * _dot(ym_ref[rs, :], wm_ref[...])
                  + g_ref[rs, d:2 * d].astype(F32) * _dot(yp_s[rs, :], wp_ref[...])
                  + g_ref[rs, 2 * d:3 * d].astype(F32) * _dot(ys_ref[rs, :], ws_ref[...]))
        xn = x_ref[rs, :] + gate * _dot(merged.astype(BF16), wo_ref[...])
        if final:
            xn = xn * lax.rsqrt(jnp.mean(xn * xn, axis=-1, keepdims=True) + EPS) * fg_ref[...]
        o_ref[rs, :] = xn


def _merge_call(xf, mod_l, y_m, y_s, out_p, out_g, wbm, wbp, wbs, wo, pw_bd, pscale, final_g,
                *, batch, final):
    n, d = xf.shape
    tm = ROW_TILE
    tiles_per_b = n // batch // tm
    halo_blocks = tm // POOL_HALO
    const = lambda i: (0, 0)
    resident = functools.partial(pl.BlockSpec, pipeline_mode=pl.Buffered(1))
    return pl.pallas_call(
        functools.partial(_merge_kernel, tiles_per_b=tiles_per_b, final=final),
        grid=(n // tm,),
        in_specs=[
            pl.BlockSpec((tm, d), lambda i: (i, 0)),
            pl.BlockSpec((1, 1, 3 * d), lambda i: (i // tiles_per_b, 0, 0)),
            pl.BlockSpec((tm, M_WIDTH), lambda i: (i, 0)),
            pl.BlockSpec((tm, S_WIDTH), lambda i: (i, 0)),
            pl.BlockSpec((tm, 2 * P_WIDTH), lambda i: (i, 0)),
            pl.BlockSpec((POOL_HALO, 2 * P_WIDTH), lambda i: (jnp.maximum(i * halo_blocks - 1, 0), 0)),
            pl.BlockSpec((tm, N_BRANCH * d), lambda i: (i, 0)),
            resident(wbm.shape, const),
            resident(wbp.shape, const),
            resident(wbs.shape, const),
            resident(wo.shape, const),
            resident(pw_bd.shape, const),
            pl.BlockSpec((1, P_WIDTH), const),
            pl.BlockSpec((1, d), const),
        ],
        out_specs=pl.BlockSpec((tm, d), lambda i: (i, 0)),
        out_shape=jax.ShapeDtypeStruct((n, d), F32),
        scratch_shapes=[
            pltpu.VMEM((POOL_PAD + POOL_HALO + tm, P_WIDTH), F32),
            pltpu.VMEM((POOL_PAD + POOL_HALO + tm, P_WIDTH), F32),
            pltpu.VMEM((tm, P_WIDTH), BF16),
        ],
        compiler_params=pltpu.CompilerParams(
            dimension_semantics=("arbitrary",), vmem_limit_bytes=V7X_VMEM_LIMIT_BYTES),
        name="pool_merge",
    )(xf, mod_l, y_m, y_s, out_p, out_p, out_g, wbm, wbp, wbs, wo, pw_bd, pscale, final_g)


GATE_COL0 = 3 * M_WIDTH


def kernel(x, c, norm_g, w_ada, b_ada, w_in, m_gate_b, conv_w, conv_b, m_norm_g, pool_w, pool_scale,
           w_br_m, w_br_p, w_br_s, gate_b, w_out, final_g):
    batch, seq, d = x.shape
    depth = w_in.shape[0]
    n = batch * seq
    assert seq % ROW_TILE == 0 and seq % (SB_SUBTILES * SB_TILE) == 0 and d % V7X_LANES == 0

    c_pad = jnp.zeros((V7X_SUBLANES, d), F32).at[:batch].set(c)
    mod = _ada_call(c_pad, w_ada, b_ada)

    xf = x.reshape(n, d)
    for l in range(depth):
        mod_l = mod[l, :batch].reshape(batch, 1, 3 * d)
        w = jnp.concatenate([w_in[l, :, :GATE_COL0], w_in[l, :, GATE_COL0 + 2 * M_HEADS:]],
                            axis=1).astype(BF16)
        w_gate = w_in[l, :, GATE_COL0:GATE_COL0 + 2 * M_HEADS]
        wgt = w_gate.T.astype(BF16)
        wgc = jnp.zeros((d, V7X_LANES), F32).at[:, :2 * M_HEADS].set(w_gate).astype(BF16)
        gbc = m_gate_b[l].reshape(2 * M_HEADS, 1)
        gbr = jnp.zeros((1, V7X_LANES), F32).at[0, :2 * M_HEADS].set(m_gate_b[l])

        out_m, out_p, out_s, out_g, out_gr, out_gc = _proj_call(
            xf, mod_l, norm_g[l].reshape(1, d), w, wgt, wgc, gbc, gbr,
            gate_b[l].reshape(1, -1), batch=batch)

        y_m = _mlstm_call(out_m, out_gr, out_gc, conv_w[l], conv_b[l].reshape(1, -1),
                          m_norm_g[l].reshape(1, -1), batch=batch)
        y_s = _sb_call(out_s, batch=batch, dead_log=SB_DEAD_LOG)

        pw_bd = jnp.zeros((P_WIDTH, P_WIDTH), F32)
        for g in range(P_GROUPS):
            sl = slice(g * P_GROUP_DIM, (g + 1) * P_GROUP_DIM)
            pw_bd = pw_bd.at[sl, sl].set(pool_w[l, g])
        xf = _merge_call(
            xf, mod_l, y_m, y_s, out_p, out_g,
            w_br_m[l].astype(BF16), w_br_p[l].astype(BF16), w_br_s[l].astype(BF16),
            w_out[l].astype(BF16), pw_bd.astype(BF16), pool_scale[l].reshape(1, -1),
            final_g.reshape(1, d), batch=batch, final=(l == depth - 1))
    return xf.reshape(batch, seq, d)
```

```python
import functools

import jax
import jax.numpy as jnp
from jax import lax
from jax.experimental import pallas as pl
from jax.experimental.pallas import tpu as pltpu

F32 = jnp.float32
BF16 = jnp.bfloat16

EPS = 1e-6
CHUNK = 256
M_HEADS = 4
M_HEAD_DIM = 128
M_WIDTH = M_HEADS * M_HEAD_DIM
CONV_K = 4
P_GROUPS = 4
P_GROUP_DIM = 64
P_WIDTH = P_GROUPS * P_GROUP_DIM
POOL_WINDOWS = (2, 4, 8, 16)
S_HEADS = 4
S_HEAD_DIM = 64
S_WIDTH = S_HEADS * S_HEAD_DIM
N_BRANCH = 3

V7X_LANES = 128
V7X_SUBLANES = 8
V7X_VMEM_LIMIT_BYTES = 56 * 1024 * 1024

ROW_TILE = 512
SB_TILE = 128
SB_SUBTILES = 4
POOL_HALO = 16
POOL_PAD = 8

SB_DEAD_LOG = -120.0


def _sigmoid(x):
    return 1.0 / (1.0 + jnp.exp(-x))


def _silu(x):
    return x * _sigmoid(x)


def _log_sigmoid(x):
    return jnp.minimum(x, 0.0) - jnp.log1p(jnp.exp(-jnp.abs(x)))


def _split_hi_lo(a):
    hi = a.astype(BF16)
    lo = (a - hi.astype(F32)).astype(BF16)
    return hi, lo


def _dot(a, b):
    return jnp.dot(a, b, preferred_element_type=F32)


def _dot_nt(a, b):
    return lax.dot_general(a, b, (((1,), (1,)), ((), ())), preferred_element_type=F32)


def _dot_tn(a, b):
    return lax.dot_general(a, b, (((0,), (0,)), ((), ())), preferred_element_type=F32)


def _ada_kernel(c_ref, w_ref, b_ref, o_ref):
    o_ref[0] = jnp.dot(c_ref[...], w_ref[0], preferred_element_type=F32,
                       precision=lax.Precision.HIGHEST) + b_ref[0]


def _ada_call(c_pad, w_ada, b_ada):
    depth, d, n3 = w_ada.shape
    tn = 1024
    return pl.pallas_call(
        _ada_kernel,
        grid=(depth, n3 // tn),
        in_specs=[
            pl.BlockSpec((c_pad.shape[0], d), lambda l, j: (0, 0)),
            pl.BlockSpec((1, d, tn), lambda l, j: (l, 0, j)),
            pl.BlockSpec((1, 1, tn), lambda l, j: (l, 0, j)),
        ],
        out_specs=pl.BlockSpec((1, c_pad.shape[0], tn), lambda l, j: (l, 0, j)),
        out_shape=jax.ShapeDtypeStruct((depth, c_pad.shape[0], n3), F32),
        compiler_params=pltpu.CompilerParams(
            dimension_semantics=("arbitrary", "arbitrary")),
        name="ada_mod",
    )(c_pad, w_ada, b_ada.reshape(depth, 1, n3))


W_OFF_M = 0
W_OFF_P = 5 * M_WIDTH
W_OFF_S = W_OFF_P + 2 * P_WIDTH
W_OFF_G = W_OFF_S + 4 * S_WIDTH
N_GATE_COLS = 3 * M_HEADS
GATE_COL0 = 3 * M_WIDTH


def _proj_kernel(x_ref, mod_ref, ng_ref, w_ref, wgate_ref, gbr_ref, mgb_ref, cw_ref, cb_ref,
                 upper_ref, lower_ref,
                 om_ref, op_ref, os_ref, og_ref, ogr_ref, ogc_ref,
                 hb_ref, hist_ref, *, tiles_per_b):
    tm, d = x_ref.shape
    first_of_batch = pl.program_id(0) % tiles_per_b == 0
    g = ng_ref[...]
    scale1 = 1.0 + mod_ref[0, :, d:2 * d]
    shift = mod_ref[0, :, 0:d]
    rc = 64
    for r0 in range(0, tm, rc):
        xf = x_ref[r0:r0 + rc, :]
        y = xf * lax.rsqrt(jnp.mean(xf * xf, axis=-1, keepdims=True) + EPS)
        y = y * g
        hb_ref[r0:r0 + rc, :] = (y * scale1 + shift).astype(BF16)

    def project(c0, width):
        return _dot(hb_ref[...], w_ref[:, c0:c0 + width])

    def conv_silu(y, c0):
        cs = slice(c0, c0 + y.shape[1])
        hist = jnp.where(first_of_batch, 0.0, hist_ref[:, cs])
        row8 = lax.broadcasted_iota(jnp.int32, hist.shape, 0)
        acc = cb_ref[:, cs] + cw_ref[CONV_K - 1:CONV_K, cs] * y
        for s in range(1, CONV_K):
            yr = pltpu.roll(y, s, 0)
            top = jnp.where(row8 < s, pltpu.roll(hist, s, 0), yr[0:V7X_SUBLANES])
            shifted = jnp.concatenate([top, yr[V7X_SUBLANES:]], axis=0)
            acc = acc + cw_ref[CONV_K - 1 - s:CONV_K - s, cs] * shifted
        hist_ref[:, cs] = y[tm - V7X_SUBLANES:tm]
        return _silu(acc)

    gc = _dot(hb_ref[...], wgate_ref[0].astype(BF16)) + gbr_ref[...]
    om_ref[:, 0:M_WIDTH] = conv_silu(project(0, M_WIDTH), 0).astype(BF16)
    om_ref[:, M_WIDTH:2 * M_WIDTH] = (
        conv_silu(project(M_WIDTH, M_WIDTH), M_WIDTH) * (M_HEAD_DIM ** -0.5)).astype(BF16)

    is_i_col = lax.broadcasted_iota(jnp.int32, gc.shape, 1) < M_HEADS
    vc = jnp.where(is_i_col, gc, _log_sigmoid(gc))
    vr = vc.T[0:2 * M_HEADS, :]
    is_i_row = lax.broadcasted_iota(jnp.int32, vr.shape, 0) < M_HEADS
    hi, lo = _split_hi_lo(vr)
    csr = _dot(hi, upper_ref[...]) + _dot(lo, upper_ref[...])
    hi, lo = _split_hi_lo(vc)
    csc = _dot(lower_ref[...], hi) + _dot(lower_ref[...], lo)
    outr = jnp.where(is_i_row, vr, csr)
    for ci in range(tm // CHUNK):
        ogr_ref[ci] = outr[:, ci * CHUNK:(ci + 1) * CHUNK]
    colg = jnp.where(is_i_col, vc, csc)
    run = colg - pltpu.roll(colg, V7X_LANES - M_HEADS, 1)
    row_in_chunk = lax.broadcasted_iota(jnp.int32, run.shape, 0) & (CHUNK - 1)
    step = 1
    while step < CHUNK:
        run = jnp.maximum(run, jnp.where(row_in_chunk >= step, pltpu.roll(run, step, 0), -jnp.inf))
        step *= 2
    lane = lax.broadcasted_iota(jnp.int32, colg.shape, 1)
    ogc_ref[...] = jnp.where(lane < 2 * M_HEADS, colg,
                             jnp.where(lane < N_GATE_COLS, pltpu.roll(run, 2 * M_HEADS, 1), 0.0))

    plan = [(om_ref, W_OFF_M, 2 * M_WIDTH, M_WIDTH, None),
            (om_ref, W_OFF_M, 3 * M_WIDTH, M_WIDTH, _sigmoid),
            (om_ref, W_OFF_M, 4 * M_WIDTH, M_WIDTH, _silu),
            (op_ref, W_OFF_P, 0, P_WIDTH, None), (op_ref, W_OFF_P, P_WIDTH, P_WIDTH, _silu)]
    plan += [(os_ref, W_OFF_S, c0, S_WIDTH, None) for c0 in range(0, 3 * S_WIDTH, S_WIDTH)]
    plan += [(os_ref, W_OFF_S, 3 * S_WIDTH, S_WIDTH, _silu)]
    for o_ref, w0, c0, width, gate in plan:
        y = project(w0 + c0, width)
        o_ref[:, c0:c0 + width] = (y if gate is None else gate(y)).astype(o_ref.dtype)
    nch = 512
    for c0 in range(0, og_ref.shape[1], nch):
        y = project(W_OFF_G + c0, nch) + mgb_ref[:, c0:c0 + nch]
        og_ref[:, c0:c0 + nch] = _sigmoid(y).astype(og_ref.dtype)


def _chunk_prefix_matrices(tm):
    idx = jnp.arange(tm)
    same = (idx[:, None] // CHUNK) == (idx[None, :] // CHUNK)
    upper = same & (idx[:, None] <= idx[None, :])
    return upper.astype(BF16), upper.T.astype(BF16)


def _proj_call(xf, mod_l, ng, w, w_in, layer, gbr, mgb, conv_w, conv_b, *, batch):
    n, d = xf.shape
    tm = ROW_TILE
    tiles_per_b = n // batch // tm
    upper, lower = _chunk_prefix_matrices(tm)
    assert GATE_COL0 % V7X_LANES == 0
    gate_block = GATE_COL0 // V7X_LANES
    const = lambda i: (0, 0)
    resident = functools.partial(pl.BlockSpec, pipeline_mode=pl.Buffered(1))
    out_shapes = (
        jax.ShapeDtypeStruct((n, W_OFF_P - W_OFF_M), BF16),
        jax.ShapeDtypeStruct((n, W_OFF_S - W_OFF_P), BF16),
        jax.ShapeDtypeStruct((n, W_OFF_G - W_OFF_S), BF16),
        jax.ShapeDtypeStruct((n, w.shape[1] - W_OFF_G), BF16),
        jax.ShapeDtypeStruct((n // CHUNK, 2 * M_HEADS, CHUNK), F32),
        jax.ShapeDtypeStruct((n, V7X_LANES), F32),
    )
    return pl.pallas_call(
        functools.partial(_proj_kernel, tiles_per_b=tiles_per_b),
        grid=(n // tm,),
        in_specs=[
            pl.BlockSpec((tm, d), lambda i: (i, 0)),
            pl.BlockSpec((1, 1, 3 * d), lambda i: (i // tiles_per_b, 0, 0)),
            pl.BlockSpec((1, d), const),
            resident(w.shape, const),
            resident((1, d, V7X_LANES), lambda i: (layer, 0, gate_block)),
            pl.BlockSpec(gbr.shape, const),
            pl.BlockSpec(mgb.shape, const),
            pl.BlockSpec(conv_w.shape, const),
            pl.BlockSpec(conv_b.shape, const),
            resident(upper.shape, const),
            resident(lower.shape, const),
        ],
        out_specs=tuple(pl.BlockSpec((tm, s.shape[1]), lambda i: (i, 0)) for s in out_shapes[:4]) + (
            pl.BlockSpec((tm // CHUNK, 2 * M_HEADS, CHUNK), lambda i: (i, 0, 0)),
            pl.BlockSpec((tm, V7X_LANES), lambda i: (i, 0)),
        ),
        out_shape=out_shapes,
        scratch_shapes=[pltpu.VMEM((tm, d), BF16),
                        pltpu.VMEM((V7X_SUBLANES, 2 * M_WIDTH), F32)],
        compiler_params=pltpu.CompilerParams(
            dimension_semantics=("arbitrary",), vmem_limit_bytes=V7X_VMEM_LIMIT_BYTES),
        name="norm_proj",
    )(xf, mod_l, ng, w, w_in, gbr, mgb, conv_w, conv_b, upper, lower)


def _mlstm_kernel(m_ref, gr_ref, gc_ref, sel_ref, ones_ref, ng_ref, o_ref, vx_s, st_s, m_s, rep_s):
    t = m_ref.shape[0]
    j = pl.program_id(1)
    qk_w = 2 * M_WIDTH
    hd = M_HEAD_DIM
    o_off = qk_w + M_WIDTH
    z_off = o_off + M_WIDTH

    @pl.when(j == 0)
    def _():
        st_s[...] = jnp.zeros_like(st_s)
        m_s[...] = jnp.zeros_like(m_s)
        for h in range(M_HEADS):
            vx_s[h, :, hd:2 * hd] = jnp.ones((t, hd), BF16)

    for h in range(M_HEADS):
        vx_s[h, :, 0:hd] = m_ref[:, qk_w + h * hd:qk_w + (h + 1) * hd]

    gcol = gc_ref[...]
    p0 = gcol.astype(BF16)
    r1 = gcol - p0.astype(F32)
    p1 = r1.astype(BF16)
    p2 = (r1 - p1.astype(F32)).astype(BF16)
    rep_s[...] = _dot(jnp.concatenate([p0, p1, p2], axis=1), sel_ref[...])

    ti = lax.broadcasted_iota(jnp.int32, (CHUNK, CHUNK), 0)
    si = lax.broadcasted_iota(jnp.int32, (CHUNK, CHUNK), 1)
    tri = si <= ti

    def chunk_body(c, carry):
        r0 = c * CHUNK
        grow = gr_ref[c]
        ig_r = grow[0:M_HEADS, :]
        bc_r = grow[M_HEADS:2 * M_HEADS, :]
        b_end = bc_r[:, CHUNK - 1:CHUNK]
        a_r = b_end - bc_r + ig_r
        a_max = jnp.max(a_r, axis=1, keepdims=True)
        m_prev = m_s[0:M_HEADS, 0:1]
        m_new = jnp.maximum(b_end + m_prev, a_max)
        decay = jnp.exp(b_end + m_prev - m_new)
        inw = jnp.exp(a_max - m_new)
        m_s[0:M_HEADS, 0:1] = m_new
        rr = ig_r - bc_r
        rs = slice(r0, r0 + CHUNK)
        for h in range(M_HEADS):
            hs = slice(h * hd, (h + 1) * hd)
            ig = rep_s[rs, h * hd:(h + 1) * hd]
            bc = rep_s[rs, (M_HEADS + h) * hd:(M_HEADS + h + 1) * hd]
            cmax = rep_s[rs, (2 * M_HEADS + h) * hd:(2 * M_HEADS + h + 1) * hd]
            mp = m_prev[h:h + 1, :]
            q_c = m_ref[rs, hs]
            k_c = m_ref[rs, M_WIDTH + h * hd:M_WIDTH + (h + 1) * hd]
            vx_c = vx_s[h, rs, :]
            big_m = jnp.maximum(mp, cmax)
            w_inter = jnp.exp(mp - big_m)
            arg = rr[h:h + 1, :] - jnp.concatenate([big_m] * (CHUNK // hd), axis=1)
            w_intra = jnp.exp(jnp.where(tri, arg, -jnp.inf)) * _dot_nt(q_c, k_c)
            inter = _dot(q_c, st_s[h].astype(BF16))
            intra = _dot(w_intra.astype(BF16), vx_c)
            hx = jnp.concatenate([w_inter, w_inter], axis=1) * inter + intra
            hout = hx[:, 0:hd] / jnp.maximum(jnp.abs(hx[:, hd:2 * hd]), jnp.exp(-(bc + big_m)))
            hg = hout * m_ref[rs, o_off + h * hd:o_off + (h + 1) * hd].astype(F32)
            hi, lo = _split_hi_lo(hg * hg)
            ssum = _dot(jnp.concatenate([hi, lo], axis=1), ones_ref[...])
            hn = hg * lax.rsqrt(ssum * (1.0 / hd) + EPS)
            o_ref[rs, hs] = (hn * ng_ref[:, hs]
                             * m_ref[rs, z_off + h * hd:z_off + (h + 1) * hd].astype(F32)).astype(o_ref.dtype)
            wa = jnp.exp(b_end[h:h + 1, :] - bc + ig - a_max[h:h + 1, :])
            kw = (k_c.astype(F32) * wa).astype(BF16)
            st_s[h] = decay[h:h + 1, :] * st_s[h] + inw[h:h + 1, :] * _dot_tn(kw, vx_c)
        return carry

    for c in range(t // CHUNK):
        chunk_body(c, 0)


def _mlstm_call(out_m, out_gr, out_gc, m_norm_g, *, batch):
    n, wm = out_m.shape
    t = ROW_TILE
    tiles_per_b = n // batch // t
    const = lambda b, j: (0, 0)
    part_lane = jnp.arange(3 * V7X_LANES)[:, None] % V7X_LANES
    sel = (part_lane == jnp.arange(N_GATE_COLS * V7X_LANES)[None, :] // V7X_LANES).astype(BF16)
    ones = jnp.ones((2 * M_HEAD_DIM, M_HEAD_DIM), BF16)
    return pl.pallas_call(
        _mlstm_kernel,
        grid=(batch, tiles_per_b),
        in_specs=[
            pl.BlockSpec((t, wm), lambda b, j: (b * tiles_per_b + j, 0)),
            pl.BlockSpec((t // CHUNK, 2 * M_HEADS, CHUNK), lambda b, j: (b * tiles_per_b + j, 0, 0)),
            pl.BlockSpec((t, V7X_LANES), lambda b, j: (b * tiles_per_b + j, 0)),
            pl.BlockSpec(sel.shape, const),
            pl.BlockSpec(ones.shape, const),
            pl.BlockSpec(m_norm_g.shape, const),
        ],
        out_specs=pl.BlockSpec((t, M_WIDTH), lambda b, j: (b * tiles_per_b + j, 0)),
        out_shape=jax.ShapeDtypeStruct((n, M_WIDTH), BF16),
        scratch_shapes=[
            pltpu.VMEM((M_HEADS, t, 2 * M_HEAD_DIM), BF16),
            pltpu.VMEM((M_HEADS, M_HEAD_DIM, 2 * M_HEAD_DIM), F32),
            pltpu.VMEM((V7X_SUBLANES, V7X_LANES), F32),
            pltpu.VMEM((t, N_GATE_COLS * V7X_LANES), F32),
        ],
        compiler_params=pltpu.CompilerParams(
            dimension_semantics=("arbitrary", "arbitrary"), vmem_limit_bytes=V7X_VMEM_LIMIT_BYTES),
        name="mlstm",
    )(out_m, out_gr, out_gc, sel, ones, m_norm_g)


def _sb_kernel(q_ref, k_ref, v_ref, z_ref, after_ref, o_ref, qm_s, carry_s, acc_s, z_s, zl_s, tot_s, *, dead_log):
    tq = tk = SB_TILE
    nsub = q_ref.shape[0] // tq
    hrows = S_HEADS * tq
    i = pl.program_id(1)
    lane_head = jnp.right_shift(lax.broadcasted_iota(jnp.int32, (1, S_WIDTH), 1),
                                S_HEAD_DIM.bit_length() - 1)

    for u in range(nsub):
        q = q_ref[u * tq:(u + 1) * tq, :] * (S_HEAD_DIM ** -0.5)
        for h in range(S_HEADS):
            qm_s[u * hrows + h * tq:u * hrows + (h + 1) * tq, :] = (
                jnp.where(lane_head == h, q, jnp.zeros_like(q)))
    carry_s[...] = jnp.zeros_like(carry_s)
    acc_s[...] = jnp.zeros_like(acc_s)

    after = after_ref[...]
    rows = nsub * hrows
    causal = (lax.broadcasted_iota(jnp.int32, (rows, tk), 1)
              < (lax.broadcasted_iota(jnp.int32, (rows, tk), 0) & (tq - 1)))

    def key_rows(u, r):
        kt = jnp.maximum(nsub * i + u - r, 0)
        return pl.ds(pl.multiple_of(kt * tk, tk), tk)

    def logits(r):
        for u in range(nsub):
            z_s[u * hrows:(u + 1) * hrows, :] = _dot_nt(
                qm_s[u * hrows:(u + 1) * hrows, :], k_ref[key_rows(u, r), :])

    def survival(diag):
        z = z_s[...]
        log_fail =-(jnp.maximum(z, 0.0) + jnp.log(1.0 + jnp.exp(-jnp.abs(z))))
        if diag:
            log_fail = jnp.where(causal, log_fail, 0.0)
        hi, lo = _split_hi_lo(log_fail)
        cum = _dot(jnp.concatenate([hi, lo], axis=1), after)
        zl = z + log_fail + cum[:, 0:tk]
        if diag:
            zl = jnp.where(causal, zl, -jnp.inf)
        zl_s[...] = zl
        tot_s[...] = cum[:, tk:2 * tk]

    def accumulate(r):
        alive = None
        for u in range(nsub):
            rs = slice(u * hrows, (u + 1) * hrows)
            carry = carry_s[rs, :]
            attn = jnp.exp(zl_s[rs, :] + carry).astype(BF16)
            vv = v_ref[key_rows(u, r), :]
            attn_wide = jnp.concatenate([attn[h * tq:(h + 1) * tq, :] for h in range(S_HEADS)], axis=1)
            v_heads = jnp.concatenate(
                [jnp.where(lane_head == h, vv, jnp.zeros_like(vv)) for h in range(S_HEADS)], axis=0)
            acc_s[u * tq:(u + 1) * tq, :] += _dot(attn_wide, v_heads)
            has_more = nsub * i + u - r > 0
            new_carry = jnp.where(has_more, carry + tot_s[rs, :], -jnp.inf)
            carry_s[rs, :] = new_carry
            alive = new_carry if alive is None else jnp.maximum(alive, new_carry)
        return jnp.max(alive)

    logits(0)
    survival(True)
    logits(1)

    last = nsub * i + nsub - 1

    def cond(state):
        r, alive = state
        return (r <= last) & (alive > dead_log)

    def body(state):
        r, _ = state
        alive = accumulate(r)
        survival(False)
        logits(r + 2)
        return r + 1, alive

    lax.while_loop(cond, body, (jnp.int32(0), jnp.float32(0.0)))

    o_ref[...] = (acc_s[...] * z_ref[...].astype(F32)).astype(o_ref.dtype)


def _sb_call(out_s, *, batch, dead_log):
    n, _ = out_s.shape
    s = n // batch
    tq = SB_SUBTILES * SB_TILE
    nq = s // tq
    kj = jnp.arange(2 * SB_TILE)[:, None] % SB_TILE
    ks = jnp.arange(2 * SB_TILE)[None, :]
    after = ((kj > ks) | (ks >= SB_TILE)).astype(BF16)
    return pl.pallas_call(
        functools.partial(_sb_kernel, dead_log=dead_log),
        grid=(batch, nq),
        in_specs=[
            pl.BlockSpec((tq, S_WIDTH), lambda b, i: (b * nq + i, 0)),
            pl.BlockSpec((s, S_WIDTH), lambda b, i: (b, 1)),
            pl.BlockSpec((s, S_WIDTH), lambda b, i: (b, 2)),
            pl.BlockSpec((tq, S_WIDTH), lambda b, i: (b * nq + i, 3)),
            pl.BlockSpec(after.shape, lambda b, i: (0, 0)),
        ],
        out_specs=pl.BlockSpec((tq, S_WIDTH), lambda b, i: (b * nq + i, 0)),
        out_shape=jax.ShapeDtypeStruct((n, S_WIDTH), BF16),
        scratch_shapes=[
            pltpu.VMEM((S_HEADS * tq, S_WIDTH), BF16),
            pltpu.VMEM((S_HEADS * tq, SB_TILE), F32),
            pltpu.VMEM((tq, S_WIDTH), F32),
            pltpu.VMEM((S_HEADS * tq, SB_TILE), F32),
            pltpu.VMEM((S_HEADS * tq, SB_TILE), F32),
            pltpu.VMEM((S_HEADS * tq, SB_TILE), F32),
        ],
        compiler_params=pltpu.CompilerParams(
            dimension_semantics=("arbitrary", "arbitrary"), vmem_limit_bytes=V7X_VMEM_LIMIT_BYTES),
        name="stick_breaking",
    )(out_s, out_s, out_s, out_s, after)


def _merge_kernel(x_ref, mod_ref, ym_ref, ys_ref, p_ref, pprev_ref, g_ref,
                  wm_ref, wp_ref, ws_ref, wo_ref, pw_ref, ps_ref, fg_ref,
                  o_ref, pa_s, pb_s, yp_s, *, tiles_per_b, final):
    tm, d = x_ref.shape
    i = pl.program_id(0)
    jb = i % tiles_per_b
    top = POOL_PAD + POOL_HALO

    pa_s[0:POOL_PAD, :] = jnp.zeros((POOL_PAD, P_WIDTH), F32)
    pb_s[0:POOL_PAD, :] = jnp.zeros((POOL_PAD, P_WIDTH), F32)
    halo = pprev_ref[:, 0:P_WIDTH].astype(F32)
    pa_s[POOL_PAD:top, :] = jnp.where(jb == 0, jnp.zeros_like(halo), halo)
    u = p_ref[:, 0:P_WIDTH].astype(F32)
    pa_s[top:top + tm, :] = u
    nrow = POOL_HALO + tm
    lane_group = jnp.right_shift(lax.broadcasted_iota(jnp.int32, (1, P_WIDTH), 1),
                                 P_GROUP_DIM.bit_length() - 1)
    src, dst = pa_s, pb_s
    wsum = None
    for level, shift in enumerate((1, 2, 4, 8)):
        summed = src[POOL_PAD:POOL_PAD + nrow, :] + src[POOL_PAD - shift:POOL_PAD - shift + nrow, :]
        dst[POOL_PAD:POOL_PAD + nrow, :] = summed
        cur = dst[top:top + tm, :]
        wsum = cur if wsum is None else jnp.where(lane_group >= level, cur, wsum)
        src, dst = dst, src
    window = jnp.left_shift(2, lane_group).astype(F32)
    tpos = (jb * tm + lax.broadcasted_iota(jnp.int32, (tm, 1), 0) + 1).astype(F32)
    count = jnp.minimum(tpos, window)
    pooled = wsum / count - u
    pm = _dot(pooled.astype(BF16), pw_ref[...])
    yp_s[...] = (pm * ps_ref[...] * p_ref[:, P_WIDTH:2 * P_WIDTH].astype(F32)).astype(BF16)

    gate = mod_ref[0, :, 2 * d:3 * d]
    rows = 256
    for r0 in range(0, tm, rows):
        rs = slice(r0, r0 + rows)
        merged = (g_ref[rs, 0:d].astype(F32) * _dot(ym_ref[rs, :], wm_ref[...])
                  + g_ref[rs, d:2 * d].astype(F32) * _dot(yp_s[rs, :], wp_ref[...])
                  + g_ref[rs, 2 * d:3 * d].astype(F32) * _dot(ys_ref[rs, :], ws_ref[...]))
        xn = x_ref[rs, :] + gate * _dot(merged.astype(BF16), wo_ref[...])
        if final:
            xn = xn * lax.rsqrt(jnp.mean(xn * xn, axis=-1, keepdims=True) + EPS) * fg_ref[...]
        o_ref[rs, :] = xn


def _merge_call(xf, mod_l, y_m, y_s, out_p, out_g, wbm, wbp, wbs, wo, pw_bd, pscale, final_g,
                *, batch, final):
    n, d = xf.shape
    tm = ROW_TILE
    tiles_per_b = n // batch // tm
    halo_blocks = tm // POOL_HALO
    const = lambda i: (0, 0)
    resident = functools.partial(pl.BlockSpec, pipeline_mode=pl.Buffered(1))
    return pl.pallas_call(
        functools.partial(_merge_kernel, tiles_per_b=tiles_per_b, final=final),
        grid=(n // tm,),
        in_specs=[
            pl.BlockSpec((tm, d), lambda i: (i, 0)),
            pl.BlockSpec((1, 1, 3 * d), lambda i: (i // tiles_per_b, 0, 0)),
            pl.BlockSpec((tm, M_WIDTH), lambda i: (i, 0)),
            pl.BlockSpec((tm, S_WIDTH), lambda i: (i, 0)),
            pl.BlockSpec((tm, 2 * P_WIDTH), lambda i: (i, 0)),
            pl.BlockSpec((POOL_HALO, 2 * P_WIDTH), lambda i: (jnp.maximum(i * halo_blocks - 1, 0), 0)),
            pl.BlockSpec((tm, N_BRANCH * d), lambda i: (i, 0)),
            resident(wbm.shape, const),
            resident(wbp.shape, const),
            resident(wbs.shape, const),
            resident(wo.shape, const),
            resident(pw_bd.shape, const),
            pl.BlockSpec((1, P_WIDTH), const),
            pl.BlockSpec((1, d), const),
        ],
        out_specs=pl.BlockSpec((tm, d), lambda i: (i, 0)),
        out_shape=jax.ShapeDtypeStruct((n, d), F32),
        scratch_shapes=[
            pltpu.VMEM((POOL_PAD + POOL_HALO + tm, P_WIDTH), F32),
            pltpu.VMEM((POOL_PAD + POOL_HALO + tm, P_WIDTH), F32),
            pltpu.VMEM((tm, P_WIDTH), BF16),
        ],
        compiler_params=pltpu.CompilerParams(
            dimension_semantics=("arbitrary",), vmem_limit_bytes=V7X_VMEM_LIMIT_BYTES),
        name="pool_merge",
    )(xf, mod_l, y_m, y_s, out_p, out_p, out_g, wbm, wbp, wbs, wo, pw_bd, pscale, final_g)


def kernel(x, c, norm_g, w_ada, b_ada, w_in, m_gate_b, conv_w, conv_b, m_norm_g, pool_w, pool_scale,
           w_br_m, w_br_p, w_br_s, gate_b, w_out, final_g):
    batch, seq, d = x.shape
    depth = w_in.shape[0]
    n = batch * seq
    assert seq % ROW_TILE == 0 and seq % (SB_SUBTILES * SB_TILE) == 0 and d % V7X_LANES == 0

    c_pad = jnp.zeros((V7X_SUBLANES, d), F32).at[:batch].set(c)
    mod = _ada_call(c_pad, w_ada, b_ada)

    xf = x.reshape(n, d)
    for l in range(depth):
        mod_l = mod[l, :batch].reshape(batch, 1, 3 * d)
        w = jnp.concatenate([w_in[l, :, :GATE_COL0], w_in[l, :, GATE_COL0 + 2 * M_HEADS:]],
                            axis=1).astype(BF16)
        gbr = jnp.zeros((1, V7X_LANES), F32).at[0, :2 * M_HEADS].set(m_gate_b[l])

        out_m, out_p, out_s, out_g, out_gr, out_gc = _proj_call(
            xf, mod_l, norm_g[l].reshape(1, d), w, w_in, l, gbr, gate_b[l].reshape(1, -1),
            conv_w[l], conv_b[l].reshape(1, -1), batch=batch)

        y_m = _mlstm_call(out_m, out_gr, out_gc, m_norm_g[l].reshape(1, -1), batch=batch)
        y_s = _sb_call(out_s, batch=batch, dead_log=SB_DEAD_LOG)

        pw_bd = jnp.zeros((P_WIDTH, P_WIDTH), F32)
        for g in range(P_GROUPS):
            sl = slice(g * P_GROUP_DIM, (g + 1) * P_GROUP_DIM)
            pw_bd = pw_bd.at[sl, sl].set(pool_w[l, g])
        xf = _merge_call(
            xf, mod_l, y_m, y_s, out_p, out_g,
            w_br_m[l].astype(BF16), w_br_p[l].astype(BF16), w_br_s[l].astype(BF16),
            w_out[l].astype(BF16), pw_bd.astype(BF16), pool_scale[l].reshape(1, -1),
            final_g.reshape(1, d), batch=batch, final=(l == depth - 1))
    return xf.reshape(batch, seq, d)
```

```python
import functools

import jax
import jax.numpy as jnp
from jax import lax
from jax.experimental import pallas as pl
from jax.experimental.pallas import tpu as pltpu

F32 = jnp.float32
BF16 = jnp.bfloat16

EPS = 1e-6
CHUNK = 256
M_HEADS = 4
M_HEAD_DIM = 128
M_WIDTH = M_HEADS * M_HEAD_DIM
CONV_K = 4
P_GROUPS = 4
P_GROUP_DIM = 64
P_WIDTH = P_GROUPS * P_GROUP_DIM
POOL_WINDOWS = (2, 4, 8, 16)
S_HEADS = 4
S_HEAD_DIM = 64
S_WIDTH = S_HEADS * S_HEAD_DIM
N_BRANCH = 3

V7X_LANES = 128
V7X_SUBLANES = 8
V7X_VMEM_LIMIT_BYTES = 56 * 1024 * 1024

ROW_TILE = 512
SB_TILE = 128
SB_SUBTILES = 4
POOL_HALO = 16
POOL_PAD = 8

SB_DEAD_LOG = -120.0


def _sigmoid(x):
    return 1.0 / (1.0 + jnp.exp(-x))


def _silu(x):
    return x * _sigmoid(x)


def _log_sigmoid(x):
    return jnp.minimum(x, 0.0) - jnp.log1p(jnp.exp(-jnp.abs(x)))


def _split_hi_lo(a):
    hi = a.astype(BF16)
    lo = (a - hi.astype(F32)).astype(BF16)
    return hi, lo


def _dot(a, b):
    return jnp.dot(a, b, preferred_element_type=F32)


def _dot_nt(a, b):
    return lax.dot_general(a, b, (((1,), (1,)), ((), ())), preferred_element_type=F32)


def _dot_tn(a, b):
    return lax.dot_general(a, b, (((0,), (0,)), ((), ())), preferred_element_type=F32)


def _ada_kernel(c_ref, w_ref, b_ref, o_ref):
    o_ref[0] = jnp.dot(c_ref[...], w_ref[0], preferred_element_type=F32,
                       precision=lax.Precision.HIGHEST) + b_ref[0]


def _ada_call(c_pad, w_ada, b_ada):
    depth, d, n3 = w_ada.shape
    tn = 1024
    return pl.pallas_call(
        _ada_kernel,
        grid=(depth, n3 // tn),
        in_specs=[
            pl.BlockSpec((c_pad.shape[0], d), lambda l, j: (0, 0)),
            pl.BlockSpec((1, d, tn), lambda l, j: (l, 0, j)),
            pl.BlockSpec((1, 1, tn), lambda l, j: (l, 0, j)),
        ],
        out_specs=pl.BlockSpec((1, c_pad.shape[0], tn), lambda l, j: (l, 0, j)),
        out_shape=jax.ShapeDtypeStruct((depth, c_pad.shape[0], n3), F32),
        compiler_params=pltpu.CompilerParams(
            dimension_semantics=("arbitrary", "arbitrary")),
        name="ada_mod",
    )(c_pad, w_ada, b_ada.reshape(depth, 1, n3))


W_OFF_M = 0
W_OFF_P = 5 * M_WIDTH
W_OFF_S = W_OFF_P + 2 * P_WIDTH
W_OFF_G = W_OFF_S + 4 * S_WIDTH
N_GATE_COLS = 3 * M_HEADS
GATE_COL0 = 3 * M_WIDTH


def _proj_kernel(x_ref, mod_ref, ng_ref, w_ref, wgate_ref, gbr_ref, mgb_ref, cw_ref, cb_ref,
                 upper_ref, lower_ref,
                 om_ref, op_ref, os_ref, og_ref, ogr_ref, ogc_ref,
                 hb_ref, hist_ref, *, tiles_per_b):
    tm, d = x_ref.shape
    first_of_batch = pl.program_id(0) % tiles_per_b == 0
    g = ng_ref[...]
    scale1 = 1.0 + mod_ref[0, :, d:2 * d]
    shift = mod_ref[0, :, 0:d]
    rc = 64
    for r0 in range(0, tm, rc):
        xf = x_ref[r0:r0 + rc, :]
        y = xf * lax.rsqrt(jnp.mean(xf * xf, axis=-1, keepdims=True) + EPS)
        y = y * g
        hb_ref[r0:r0 + rc, :] = (y * scale1 + shift).astype(BF16)

    def project(c0, width):
        return _dot(hb_ref[...], w_ref[:, c0:c0 + width])

    def conv_silu_chunk(c0, width):
        cs = slice(c0, c0 + width)
        y = project(c0, width)
        hist = jnp.where(first_of_batch, 0.0, hist_ref[:, cs])
        row8 = lax.broadcasted_iota(jnp.int32, hist.shape, 0)
        acc = cb_ref[:, cs] + cw_ref[CONV_K - 1:CONV_K, cs] * y
        for s in range(1, CONV_K):
            yr = pltpu.roll(y, s, 0)
            top = jnp.where(row8 < s, pltpu.roll(hist, s, 0), yr[0:V7X_SUBLANES])
            shifted = jnp.concatenate([top, yr[V7X_SUBLANES:]], axis=0)
            acc = acc + cw_ref[CONV_K - 1 - s:CONV_K - s, cs] * shifted
        hist_ref[:, cs] = y[tm - V7X_SUBLANES:tm]
        act = _silu(acc)
        if c0 >= M_WIDTH:
            act = act * (M_HEAD_DIM ** -0.5)
        om_ref[:, cs] = act.astype(BF16)

    gc = _dot(hb_ref[...], wgate_ref[...]) + gbr_ref[...]

    def gate_chain():
        is_i_col = lax.broadcasted_iota(jnp.int32, gc.shape, 1) < M_HEADS
        vc = jnp.where(is_i_col, gc, _log_sigmoid(gc))
        vr = vc.T[0:2 * M_HEADS, :]
        is_i_row = lax.broadcasted_iota(jnp.int32, vr.shape, 0) < M_HEADS
        hi, lo = _split_hi_lo(vr)
        csr = _dot(hi, upper_ref[...]) + _dot(lo, upper_ref[...])
        hi, lo = _split_hi_lo(vc)
        csc = _dot(lower_ref[...], hi) + _dot(lower_ref[...], lo)
        outr = jnp.where(is_i_row, vr, csr)
        for ci in range(tm // CHUNK):
            ogr_ref[ci] = outr[:, ci * CHUNK:(ci + 1) * CHUNK]
        colg = jnp.where(is_i_col, vc, csc)
        run = colg - pltpu.roll(colg, V7X_LANES - M_HEADS, 1)
        row_in_chunk = lax.broadcasted_iota(jnp.int32, run.shape, 0) & (CHUNK - 1)
        step = 1
        while step < CHUNK:
            run = jnp.maximum(run, jnp.where(row_in_chunk >= step, pltpu.roll(run, step, 0), -jnp.inf))
            step *= 2
        lane = lax.broadcasted_iota(jnp.int32, colg.shape, 1)
        ogc_ref[...] = jnp.where(lane < 2 * M_HEADS, colg,
                                 jnp.where(lane < N_GATE_COLS, pltpu.roll(run, 2 * M_HEADS, 1), 0.0))

    plan = [(om_ref, W_OFF_M, 2 * M_WIDTH, M_WIDTH, None),
            (om_ref, W_OFF_M, 3 * M_WIDTH, M_WIDTH, _sigmoid),
            (om_ref, W_OFF_M, 4 * M_WIDTH, M_WIDTH, _silu),
            (op_ref, W_OFF_P, 0, P_WIDTH, None), (op_ref, W_OFF_P, P_WIDTH, P_WIDTH, _silu)]
    plan += [(os_ref, W_OFF_S, c0, S_WIDTH, None) for c0 in range(0, 3 * S_WIDTH, S_WIDTH)]
    plan += [(os_ref, W_OFF_S, 3 * S_WIDTH, S_WIDTH, _silu)]
    nch = 512
    plan += [(og_ref, W_OFF_G, c0, nch, _sigmoid) for c0 in range(0, og_ref.shape[1], nch)]
    conv_silu_chunk(0, M_WIDTH)
    conv_silu_chunk(M_WIDTH, M_WIDTH)
    gate_chain()
    for o_ref, w0, c0, width, gate in plan:
        y = project(w0 + c0, width)
        if o_ref is og_ref:
            y = y + mgb_ref[:, c0:c0 + width]
        o_ref[:, c0:c0 + width] = (y if gate is None else gate(y)).astype(o_ref.dtype)


def _chunk_prefix_matrices(tm):
    idx = jnp.arange(tm)
    same = (idx[:, None] // CHUNK) == (idx[None, :] // CHUNK)
    upper = same & (idx[:, None] <= idx[None, :])
    return upper.astype(BF16), upper.T.astype(BF16)


def _wprep_kernel(a_ref, b_ref, w_ref, wg_ref, *, gate_step):
    j = pl.program_id(1)
    ngate = b_ref.shape[1]

    @pl.when(j < gate_step)
    def _():
        w_ref[0] = a_ref[0].astype(BF16).T

    @pl.when(j >= gate_step)
    def _():
        w_ref[0] = jnp.concatenate([a_ref[0, ngate:, :], b_ref[0]], axis=0).astype(BF16).T

    @pl.when(j == gate_step)
    def _():
        wg_ref[0] = a_ref[0, 0:V7X_LANES, :].astype(BF16).T


def _wprep_call(w_in):
    depth, d, n_in = w_in.shape
    width = 512
    ngate = 2 * M_HEADS
    n_out = n_in - ngate
    assert GATE_COL0 % width == 0 and n_out % width == 0 and width % ngate == 0
    wt = jnp.swapaxes(w_in, 1, 2)
    return pl.pallas_call(
        functools.partial(_wprep_kernel, gate_step=GATE_COL0 // width),
        grid=(depth, n_out // width),
        in_specs=[
            pl.BlockSpec((1, width, d), lambda l, j: (l, j, 0)),
            pl.BlockSpec((1, ngate, d), lambda l, j: (l, (j + 1) * (width // ngate), 0)),
        ],
        out_specs=(
            pl.BlockSpec((1, d, width), lambda l, j: (l, 0, j)),
            pl.BlockSpec((1, d, V7X_LANES), lambda l, j: (l, 0, 0)),
        ),
        out_shape=(jax.ShapeDtypeStruct((depth, d, n_out), BF16),
                   jax.ShapeDtypeStruct((depth, d, V7X_LANES), BF16)),
        compiler_params=pltpu.CompilerParams(dimension_semantics=("arbitrary", "arbitrary")),
        name="weight_prep",
    )(wt, wt)


def _proj_call(xf, mod_l, ng, w_all, wgate_all, layer, gbr, mgb, conv_w, conv_b, *, batch):
    n, d = xf.shape
    tm = ROW_TILE
    tiles_per_b = n // batch // tm
    upper, lower = _chunk_prefix_matrices(tm)
    n_out = w_all.shape[2]
    const = lambda i: (0, 0)
    resident = functools.partial(pl.BlockSpec, pipeline_mode=pl.Buffered(1))
    out_shapes = (
        jax.ShapeDtypeStruct((n, W_OFF_P - W_OFF_M), BF16),
        jax.ShapeDtypeStruct((n, W_OFF_S - W_OFF_P), BF16),
        jax.ShapeDtypeStruct((n, W_OFF_G - W_OFF_S), BF16),
        jax.ShapeDtypeStruct((n, n_out - W_OFF_G), BF16),
        jax.ShapeDtypeStruct((n // CHUNK, 2 * M_HEADS, CHUNK), F32),
        jax.ShapeDtypeStruct((n, V7X_LANES), F32),
    )
    return pl.pallas_call(
        functools.partial(_proj_kernel, tiles_per_b=tiles_per_b),
        grid=(n // tm,),
        in_specs=[
            pl.BlockSpec((tm, d), lambda i: (i, 0)),
            pl.BlockSpec((1, 1, 3 * d), lambda i: (i // tiles_per_b, 0, 0)),
            pl.BlockSpec((1, d), const),
            resident((None, d, n_out), lambda i: (layer, 0, 0)),
            resident((None, d, V7X_LANES), lambda i: (layer, 0, 0)),
            pl.BlockSpec(gbr.shape, const),
            pl.BlockSpec(mgb.shape, const),
            pl.BlockSpec(conv_w.shape, const),
            pl.BlockSpec(conv_b.shape, const),
            resident(upper.shape, const),
            resident(lower.shape, const),
        ],
        out_specs=tuple(pl.BlockSpec((tm, s.shape[1]), lambda i: (i, 0)) for s in out_shapes[:4]) + (
            pl.BlockSpec((tm // CHUNK, 2 * M_HEADS, CHUNK), lambda i: (i, 0, 0)),
            pl.BlockSpec((tm, V7X_LANES), lambda i: (i, 0)),
        ),
        out_shape=out_shapes,
        scratch_shapes=[pltpu.VMEM((tm, d), BF16),
                        pltpu.VMEM((V7X_SUBLANES, 2 * M_WIDTH), F32)],
        compiler_params=pltpu.CompilerParams(
            dimension_semantics=("arbitrary",), vmem_limit_bytes=V7X_VMEM_LIMIT_BYTES),
        name="norm_proj",
    )(xf, mod_l, ng, w_all, wgate_all, gbr, mgb, conv_w, conv_b, upper, lower)


def _mlstm_kernel(m_ref, gr_ref, gc_ref, sel_ref, ones_ref, ng_ref, o_ref, vx_s, st_s, m_s, rep_s):
    t = m_ref.shape[0]
    j = pl.program_id(1)
    qk_w = 2 * M_WIDTH
    hd = M_HEAD_DIM
    o_off = qk_w + M_WIDTH
    z_off = o_off + M_WIDTH

    @pl.when(j == 0)
    def _():
        st_s[...] = jnp.zeros_like(st_s)
        m_s[...] = jnp.zeros_like(m_s)
        for h in range(M_HEADS):
            vx_s[h, :, hd:2 * hd] = jnp.ones((t, hd), BF16)

    for h in range(M_HEADS):
        vx_s[h, :, 0:hd] = m_ref[:, qk_w + h * hd:qk_w + (h + 1) * hd]

    gcol = gc_ref[...]
    p0 = gcol.astype(BF16)
    r1 = gcol - p0.astype(F32)
    p1 = r1.astype(BF16)
    p2 = (r1 - p1.astype(F32)).astype(BF16)
    rep_s[...] = _dot(jnp.concatenate([p0, p1, p2], axis=1), sel_ref[...])

    ti = lax.broadcasted_iota(jnp.int32, (CHUNK, CHUNK), 0)
    si = lax.broadcasted_iota(jnp.int32, (CHUNK, CHUNK), 1)
    tri = si <= ti

    def chunk_body(c, carry):
        r0 = c * CHUNK
        grow = gr_ref[c]
        ig_r = grow[0:M_HEADS, :]
        bc_r = grow[M_HEADS:2 * M_HEADS, :]
        b_end = bc_r[:, CHUNK - 1:CHUNK]
        a_r = b_end - bc_r + ig_r
        a_max = jnp.max(a_r, axis=1, keepdims=True)
        m_prev = m_s[0:M_HEADS, 0:1]
        m_new = jnp.maximum(b_end + m_prev, a_max)
        decay = jnp.exp(b_end + m_prev - m_new)
        inw = jnp.exp(a_max - m_new)
        m_s[0:M_HEADS, 0:1] = m_new
        rr = ig_r - bc_r
        rs = slice(r0, r0 + CHUNK)
        for h in range(M_HEADS):
            hs = slice(h * hd, (h + 1) * hd)
            ig = rep_s[rs, h * hd:(h + 1) * hd]
            bc = rep_s[rs, (M_HEADS + h) * hd:(M_HEADS + h + 1) * hd]
            cmax = rep_s[rs, (2 * M_HEADS + h) * hd:(2 * M_HEADS + h + 1) * hd]
            mp = m_prev[h:h + 1, :]
            q_c = m_ref[rs, hs]
            k_c = m_ref[rs, M_WIDTH + h * hd:M_WIDTH + (h + 1) * hd]
            vx_c = vx_s[h, rs, :]
            big_m = jnp.maximum(mp, cmax)
            w_inter = jnp.exp(mp - big_m)
            arg = rr[h:h + 1, :] - jnp.concatenate([big_m] * (CHUNK // hd), axis=1)
            w_intra = jnp.exp(jnp.where(tri, arg, -jnp.inf)) * _dot_nt(q_c, k_c)
            inter = _dot(q_c, st_s[h].astype(BF16))
            intra = _dot(w_intra.astype(BF16), vx_c)
            hx = jnp.concatenate([w_inter, w_inter], axis=1) * inter + intra
            hout = hx[:, 0:hd] / jnp.maximum(jnp.abs(hx[:, hd:2 * hd]), jnp.exp(-(bc + big_m)))
            hg = hout * m_ref[rs, o_off + h * hd:o_off + (h + 1) * hd].astype(F32)
            hi, lo = _split_hi_lo(hg * hg)
            ssum = _dot(jnp.concatenate([hi, lo], axis=1), ones_ref[...])
            hn = hg * lax.rsqrt(ssum * (1.0 / hd) + EPS)
            o_ref[rs, hs] = (hn * ng_ref[:, hs]
                             * m_ref[rs, z_off + h * hd:z_off + (h + 1) * hd].astype(F32)).astype(o_ref.dtype)
            wa = jnp.exp(b_end[h:h + 1, :] - bc + ig - a_max[h:h + 1, :])
            kw = (k_c.astype(F32) * wa).astype(BF16)
            st_s[h] = decay[h:h + 1, :] * st_s[h] + inw[h:h + 1, :] * _dot_tn(kw, vx_c)
        return carry

    for c in range(t // CHUNK):
        chunk_body(c, 0)


def _mlstm_call(out_m, out_gr, out_gc, m_norm_g, *, batch):
    n, wm = out_m.shape
    t = ROW_TILE
    tiles_per_b = n // batch // t
    const = lambda b, j: (0, 0)
    part_lane = jnp.arange(3 * V7X_LANES)[:, None] % V7X_LANES
    sel = (part_lane == jnp.arange(N_GATE_COLS * V7X_LANES)[None, :] // V7X_LANES).astype(BF16)
    ones = jnp.ones((2 * M_HEAD_DIM, M_HEAD_DIM), BF16)
    return pl.pallas_call(
        _mlstm_kernel,
        grid=(batch, tiles_per_b),
        in_specs=[
            pl.BlockSpec((t, wm), lambda b, j: (b * tiles_per_b + j, 0)),
            pl.BlockSpec((t // CHUNK, 2 * M_HEADS, CHUNK), lambda b, j: (b * tiles_per_b + j, 0, 0)),
            pl.BlockSpec((t, V7X_LANES), lambda b, j: (b * tiles_per_b + j, 0)),
            pl.BlockSpec(sel.shape, const),
            pl.BlockSpec(ones.shape, const),
            pl.BlockSpec(m_norm_g.shape, const),
        ],
        out_specs=pl.BlockSpec((t, M_WIDTH), lambda b, j: (b * tiles_per_b + j, 0)),
        out_shape=jax.ShapeDtypeStruct((n, M_WIDTH), BF16),
        scratch_shapes=[
            pltpu.VMEM((M_HEADS, t, 2 * M_HEAD_DIM), BF16),
            pltpu.VMEM((M_HEADS, M_HEAD_DIM, 2 * M_HEAD_DIM), F32),
            pltpu.VMEM((V7X_SUBLANES, V7X_LANES), F32),
            pltpu.VMEM((t, N_GATE_COLS * V7X_LANES), F32),
        ],
        compiler_params=pltpu.CompilerParams(
            dimension_semantics=("arbitrary", "arbitrary"), vmem_limit_bytes=V7X_VMEM_LIMIT_BYTES),
        name="mlstm",
    )(out_m, out_gr, out_gc, sel, ones, m_norm_g)


def _sb_kernel(q_ref, k_ref, v_ref, z_ref, after_ref, o_ref, qm_s, carry_s, acc_s, z_s, zl_s, tot_s, *, dead_log):
    tq = tk = SB_TILE
    nsub = q_ref.shape[0] // tq
    hrows = S_HEADS * tq
    i = pl.program_id(1)
    lane_head = jnp.right_shift(lax.broadcasted_iota(jnp.int32, (1, S_WIDTH), 1),
                                S_HEAD_DIM.bit_length() - 1)

    for u in range(nsub):
        q = q_ref[u * tq:(u + 1) * tq, :] * (S_HEAD_DIM ** -0.5)
        for h in range(S_HEADS):
            qm_s[u * hrows + h * tq:u * hrows + (h + 1) * tq, :] = (
                jnp.where(lane_head == h, q, jnp.zeros_like(q)))
    carry_s[...] = jnp.zeros_like(carry_s)
    acc_s[...] = jnp.zeros_like(acc_s)

    after = after_ref[...]
    rows = nsub * hrows
    causal = (lax.broadcasted_iota(jnp.int32, (rows, tk), 1)
              < (lax.broadcasted_iota(jnp.int32, (rows, tk), 0) & (tq - 1)))

    def key_rows(u, r):
        kt = jnp.maximum(nsub * i + u - r, 0)
        return pl.ds(pl.multiple_of(kt * tk, tk), tk)

    def logits(r):
        for u in range(nsub):
            z_s[u * hrows:(u + 1) * hrows, :] = _dot_nt(
                qm_s[u * hrows:(u + 1) * hrows, :], k_ref[key_rows(u, r), :])

    def survival(diag):
        z = z_s[...]
        log_fail =-(jnp.maximum(z, 0.0) + jnp.log(1.0 + jnp.exp(-jnp.abs(z))))
        if diag:
            log_fail = jnp.where(causal, log_fail, 0.0)
        hi, lo = _split_hi_lo(log_fail)
        cum = _dot(jnp.concatenate([hi, lo], axis=1), after)
        zl = z + log_fail + cum[:, 0:tk]
        if diag:
            zl = jnp.where(causal, zl, -jnp.inf)
        zl_s[...] = zl
        tot_s[...] = cum[:, tk:2 * tk]

    def accumulate(r):
        alive = None
        for u in range(nsub):
            rs = slice(u * hrows, (u + 1) * hrows)
            carry = carry_s[rs, :]
            attn = jnp.exp(zl_s[rs, :] + carry).astype(BF16)
            vv = v_ref[key_rows(u, r), :]
            attn_wide = jnp.concatenate([attn[h * tq:(h + 1) * tq, :] for h in range(S_HEADS)], axis=1)
            v_heads = jnp.concatenate(
                [jnp.where(lane_head == h, vv, jnp.zeros_like(vv)) for h in range(S_HEADS)], axis=0)
            acc_s[u * tq:(u + 1) * tq, :] += _dot(attn_wide, v_heads)
            has_more = nsub * i + u - r > 0
            new_carry = jnp.where(has_more, carry + tot_s[rs, :], -jnp.inf)
            carry_s[rs, :] = new_carry
            alive = new_carry if alive is None else jnp.maximum(alive, new_carry)
        return jnp.max(alive)

    logits(0)
    survival(True)
    logits(1)

    last = nsub * i + nsub - 1

    def cond(state):
        r, alive = state
        return (r <= last) & (alive > dead_log)

    def body(state):
        r, _ = state
        alive = accumulate(r)
        survival(False)
        logits(r + 2)
        return r + 1, alive

    lax.while_loop(cond, body, (jnp.int32(0), jnp.float32(0.0)))

    o_ref[...] = (acc_s[...] * z_ref[...].astype(F32)).astype(o_ref.dtype)


def _sb_call(out_s, *, batch, dead_log):
    n, _ = out_s.shape
    s = n // batch
    tq = SB_SUBTILES * SB_TILE
    nq = s // tq
    kj = jnp.arange(2 * SB_TILE)[:, None] % SB_TILE
    ks = jnp.arange(2 * SB_TILE)[None, :]
    after = ((kj > ks) | (ks >= SB_TILE)).astype(BF16)
    return pl.pallas_call(
        functools.partial(_sb_kernel, dead_log=dead_log),
        grid=(batch, nq),
        in_specs=[
            pl.BlockSpec((tq, S_WIDTH), lambda b, i: (b * nq + i, 0)),
            pl.BlockSpec((s, S_WIDTH), lambda b, i: (b, 1)),
            pl.BlockSpec((s, S_WIDTH), lambda b, i: (b, 2)),
            pl.BlockSpec((tq, S_WIDTH), lambda b, i: (b * nq + i, 3)),
            pl.BlockSpec(after.shape, lambda b, i: (0, 0)),
        ],
        out_specs=pl.BlockSpec((tq, S_WIDTH), lambda b, i: (b * nq + i, 0)),
        out_shape=jax.ShapeDtypeStruct((n, S_WIDTH), BF16),
        scratch_shapes=[
            pltpu.VMEM((S_HEADS * tq, S_WIDTH), BF16),
            pltpu.VMEM((S_HEADS * tq, SB_TILE), F32),
            pltpu.VMEM((tq, S_WIDTH), F32),
            pltpu.VMEM((S_HEADS * tq, SB_TILE), F32),
            pltpu.VMEM((S_HEADS * tq, SB_TILE), F32),
            pltpu.VMEM((S_HEADS * tq, SB_TILE), F32),
        ],
        compiler_params=pltpu.CompilerParams(
            dimension_semantics=("arbitrary", "arbitrary"), vmem_limit_bytes=V7X_VMEM_LIMIT_BYTES),
        name="stick_breaking",
    )(out_s, out_s, out_s, out_s, after)


def _merge_kernel(x_ref, mod_ref, ym_ref, ys_ref, p_ref, pprev_ref, g_ref,
                  wm_ref, wp_ref, ws_ref, wo_ref, pw_ref, ps_ref, fg_ref,
                  o_ref, pa_s, pb_s, yp_s, *, tiles_per_b, final):
    tm, d = x_ref.shape
    i = pl.program_id(0)
    jb = i % tiles_per_b
    top = POOL_PAD + POOL_HALO

    pa_s[0:POOL_PAD, :] = jnp.zeros((POOL_PAD, P_WIDTH), F32)
    pb_s[0:POOL_PAD, :] = jnp.zeros((POOL_PAD, P_WIDTH), F32)
    halo = pprev_ref[:, 0:P_WIDTH].astype(F32)
    pa_s[POOL_PAD:top, :] = jnp.where(jb == 0, jnp.zeros_like(halo), halo)
    u = p_ref[:, 0:P_WIDTH].astype(F32)
    pa_s[top:top + tm, :] = u
    nrow = POOL_HALO + tm
    lane_group = jnp.right_shift(lax.broadcasted_iota(jnp.int32, (1, P_WIDTH), 1),
                                 P_GROUP_DIM.bit_length() - 1)
    src, dst = pa_s, pb_s
    wsum = None
    for level, shift in enumerate((1, 2, 4, 8)):
        summed = src[POOL_PAD:POOL_PAD + nrow, :] + src[POOL_PAD - shift:POOL_PAD - shift + nrow, :]
        dst[POOL_PAD:POOL_PAD + nrow, :] = summed
        cur = dst[top:top + tm, :]
        wsum = cur if wsum is None else jnp.where(lane_group >= level, cur, wsum)
        src, dst = dst, src
    window = jnp.left_shift(2, lane_group).astype(F32)
    tpos = (jb * tm + lax.broadcasted_iota(jnp.int32, (tm, 1), 0) + 1).astype(F32)
    count = jnp.minimum(tpos, window)
    pooled = wsum / count - u
    pm = _dot(pooled.astype(BF16), pw_ref[...])
    yp_s[...] = (pm * ps_ref[...] * p_ref[:, P_WIDTH:2 * P_WIDTH].astype(F32)).astype(BF16)

    gate = mod_ref[0, :, 2 * d:3 * d]
    rows = 256
    for r0 in range(0, tm, rows):
        rs = slice(r0, r0 + rows)
        merged = (g_ref[rs, 0:d].astype(F32) * _dot(ym_ref[rs, :], wm_ref[...])
                  + g_ref[rs, d:2 * d].astype(F32) * _dot(yp_s[rs, :], wp_ref[...])
                  + g_ref[rs, 2 * d:3 * d].astype(F32) * _dot(ys_ref[rs, :], ws_ref[...]))
        xn = x_ref[rs, :] + gate * _dot(merged.astype(BF16), wo_ref[...])
        if final:
            xn = xn * lax.rsqrt(jnp.mean(xn * xn, axis=-1, keepdims=True) + EPS) * fg_ref[...]
        o_ref[rs, :] = xn


def _merge_call(xf, mod_l, y_m, y_s, out_p, out_g, wbm, wbp, wbs, wo, pw_bd, pscale, final_g,
                *, batch, final):
    n, d = xf.shape
    tm = ROW_TILE
    tiles_per_b = n // batch // tm
    halo_blocks = tm // POOL_HALO
    const = lambda i: (0, 0)
    resident = functools.partial(pl.BlockSpec, pipeline_mode=pl.Buffered(1))
    return pl.pallas_call(
        functools.partial(_merge_kernel, tiles_per_b=tiles_per_b, final=final),
        grid=(n // tm,),
        in_specs=[
            pl.BlockSpec((tm, d), lambda i: (i, 0)),
            pl.BlockSpec((1, 1, 3 * d), lambda i: (i // tiles_per_b, 0, 0)),
            pl.BlockSpec((tm, M_WIDTH), lambda i: (i, 0)),
            pl.BlockSpec((tm, S_WIDTH), lambda i: (i, 0)),
            pl.BlockSpec((tm, 2 * P_WIDTH), lambda i: (i, 0)),
            pl.BlockSpec((POOL_HALO, 2 * P_WIDTH), lambda i: (jnp.maximum(i * halo_blocks - 1, 0), 0)),
            pl.BlockSpec((tm, N_BRANCH * d), lambda i: (i, 0)),
            resident(wbm.shape, const),
            resident(wbp.shape, const),
            resident(wbs.shape, const),
            resident(wo.shape, const),
            resident(pw_bd.shape, const),
            pl.BlockSpec((1, P_WIDTH), const),
            pl.BlockSpec((1, d), const),
        ],
        out_specs=pl.BlockSpec((tm, d), lambda i: (i, 0)),
        out_shape=jax.ShapeDtypeStruct((n, d), F32),
        scratch_shapes=[
            pltpu.VMEM((POOL_PAD + POOL_HALO + tm, P_WIDTH), F32),
            pltpu.VMEM((POOL_PAD + POOL_HALO + tm, P_WIDTH), F32),
            pltpu.VMEM((tm, P_WIDTH), BF16),
        ],
        compiler_params=pltpu.CompilerParams(
            dimension_semantics=("arbitrary",), vmem_limit_bytes=V7X_VMEM_LIMIT_BYTES),
        name="pool_merge",
    )(xf, mod_l, y_m, y_s, out_p, out_p, out_g, wbm, wbp, wbs, wo, pw_bd, pscale, final_g)


def kernel(x, c, norm_g, w_ada, b_ada, w_in, m_gate_b, conv_w, conv_b, m_norm_g, pool_w, pool_scale,
           w_br_m, w_br_p, w_br_s, gate_b, w_out, final_g):
    batch, seq, d = x.shape
    depth = w_in.shape[0]
    n = batch * seq
    assert seq % ROW_TILE == 0 and seq % (SB_SUBTILES * SB_TILE) == 0 and d % V7X_LANES == 0

    c_pad = jnp.zeros((V7X_SUBLANES, d), F32).at[:batch].set(c)
    mod = _ada_call(c_pad, w_ada, b_ada)

    w_all, wgate_all = _wprep_call(w_in)

    xf = x.reshape(n, d)
    for l in range(depth):
        mod_l = mod[l, :batch].reshape(batch, 1, 3 * d)
        gbr = jnp.zeros((1, V7X_LANES), F32).at[0, :2 * M_HEADS].set(m_gate_b[l])

        out_m, out_p, out_s, out_g, out_gr, out_gc = _proj_call(
            xf, mod_l, norm_g[l].reshape(1, d), w_all, wgate_all, l, gbr, gate_b[l].reshape(1, -1),
            conv_w[l], conv_b[l].reshape(1, -1), batch=batch)

        y_m = _mlstm_call(out_m, out_gr, out_gc, m_norm_g[l].reshape(1, -1), batch=batch)
        y_s = _sb_call(out_s, batch=batch, dead_log=SB_DEAD_LOG)

        pw_bd = jnp.zeros((P_WIDTH, P_WIDTH), F32)
        for g in range(P_GROUPS):
            sl = slice(g * P_GROUP_DIM, (g + 1) * P_GROUP_DIM)
            pw_bd = pw_bd.at[sl, sl].set(pool_w[l, g])
        xf = _merge_call(
            xf, mod_l, y_m, y_s, out_p, out_g,
            w_br_m[l].astype(BF16), w_br_p[l].astype(BF16), w_br_s[l].astype(BF16),
            w_out[l].astype(BF16), pw_bd.astype(BF16), pool_scale[l].reshape(1, -1),
            final_g.reshape(1, d), batch=batch, final=(l == depth - 1))
    return xf.reshape(batch, seq, d)
```
